```python
import jax, jax.numpy as jnp
from jax import lax
import numpy as np

D_MODEL = 1024
BATCH = 2
SEQ = 8192
DEPTH = 1
DEC_BATCH = 128
DEC_SEQ = 1
PAST_LEN = 2048
PAGE_SIZE = 128

D_MIX = D_MODEL
D_ATT = D_MIX // 2
D_CHK = D_MIX - D_ATT
HEAD_DIM = 64
N_HEADS = D_ATT // HEAD_DIM
CHUNK = 128
CH_GROUP = 128
N_GROUPS = D_CHK // CH_GROUP
Q_BLOCK = 128
EPS = 1e-6
SCALE = HEAD_DIM ** -0.5
SPLIT_IDX = (D_ATT, 2 * D_ATT, 3 * D_ATT, 3 * D_ATT + N_HEADS, 4 * D_ATT + N_HEADS,
             4 * D_ATT + N_HEADS + D_CHK, 4 * D_ATT + N_HEADS + 2 * D_CHK)
D_IN = 4 * D_ATT + N_HEADS + 3 * D_CHK

kernel_name = 'hymba_fox_chunk_gmlp_decode_step'


def rms_norm(x, g):
    xf = x.astype(jnp.float32)
    y = xf * lax.rsqrt(jnp.mean(xf * xf, axis=-1, keepdims=True) + EPS)
    return (y * g.astype(jnp.float32)).astype(x.dtype)


def branch_inputs(x, g_norm, w_in, b_f, g_q, g_k, g_v):
    B, S, _ = x.shape
    h = rms_norm(x, g_norm)
    p = h @ w_in
    q, k, v, fz, za, u, gv, zc = jnp.split(p, SPLIT_IDX, axis=-1)
    q = rms_norm(q.reshape(B, S, N_HEADS, HEAD_DIM), g_q)
    k = rms_norm(k.reshape(B, S, N_HEADS, HEAD_DIM), g_k)
    v = v.reshape(B, S, N_HEADS, HEAD_DIM)
    logf = jax.nn.log_sigmoid((fz + b_f).astype(jnp.float32))
    u = u.reshape(B, S, N_GROUPS, CH_GROUP)
    gv = rms_norm(gv.reshape(B, S, N_GROUPS, CH_GROUP), g_v.reshape(N_GROUPS, CH_GROUP))
    return q, k, v, logf, za, u, gv, zc


def fox_prompt(q, k, v, logf):
    f32 = jnp.float32
    B, S, H, Dh = q.shape
    c = jnp.cumsum(logf, axis=1).transpose(0, 2, 1)
    kf = k.astype(f32)
    vf = v.astype(f32)
    qf = q.astype(f32)
    pos_k = jnp.arange(S)

    def block(i):
        start = i * Q_BLOCK
        qb = lax.dynamic_slice_in_dim(qf, start, Q_BLOCK, axis=1)
        cb = lax.dynamic_slice_in_dim(c, start, Q_BLOCK, axis=2)
        s = jnp.einsum('bqhd,bkhd->bhqk', qb, kf) * SCALE
        s = s + cb[:, :, :, None] - c[:, :, None, :]
        pos_q = start + jnp.arange(Q_BLOCK)
        s = jnp.where(pos_k[None, :] <= pos_q[:, None], s, -jnp.inf)
        pr = jax.nn.softmax(s, axis=-1)
        return jnp.einsum('bhqk,bkhd->bqhd', pr, vf)

    o = lax.map(block, jnp.arange(S // Q_BLOCK))
    return o.transpose(1, 0, 2, 3, 4).reshape(B, S, H, Dh)


def fox_sample(q, k, v, logf, k_past, v_past, logf_past):
    f32 = jnp.float32
    P = k_past.shape[1]
    T = q.shape[1]
    kk = jnp.concatenate([k_past.astype(f32), k.astype(f32)], axis=1)
    vv = jnp.concatenate([v_past.astype(f32), v.astype(f32)], axis=1)
    c = jnp.cumsum(jnp.concatenate([logf_past.astype(f32), logf], axis=1), axis=1).transpose(0, 2, 1)
    s = jnp.einsum('bqhd,bkhd->bhqk', q.astype(f32), kk) * SCALE
    s = s + c[:, :, P:, None] - c[:, :, None, :]
    mask = jnp.arange(P + T)[None, :] <= (P + jnp.arange(T))[:, None]
    s = jnp.where(mask, s, -jnp.inf)
    pr = jax.nn.softmax(s, axis=-1)
    return jnp.einsum('bhqk,bkhd->bqhd', pr, vv)


def chunk_mix(gv, w_s, b_s):
    B, S, G, C = gv.shape
    n = -(-S // CHUNK)
    vp = jnp.pad(gv, ((0, 0), (0, n * CHUNK - S), (0, 0), (0, 0))).reshape(B, n, CHUNK, G, C)
    w = w_s * jnp.tril(jnp.ones((CHUNK, CHUNK), w_s.dtype))
    s = jnp.einsum('gts,bnsgc->bntgc', w, vp) + b_s.T[None, None, :, :, None]
    return s.reshape(B, n * CHUNK, G, C)[:, :S]


def merge(x, attn, za, mix, u, zc, w_out):
    B, S, _ = x.shape
    a = attn.astype(x.dtype).reshape(B, S, D_ATT) * jax.nn.silu(za)
    m = (u * mix.astype(x.dtype)).reshape(B, S, D_CHK) * jax.nn.silu(zc)
    return x + jnp.concatenate([a, m], axis=-1) @ w_out


def setup_inputs(seed: int = 0) -> dict:
    key = jax.random.key(seed)
    ks = jax.random.split(key, 16)
    f32 = jnp.float32
    n_pages = PAST_LEN // PAGE_SIZE
    n_used = DEC_BATCH * n_pages
    n_phys = n_used + (n_used + 3) // 4
    nrm = jax.random.normal
    x_prompt = nrm(ks[0], (BATCH, SEQ, D_MODEL), f32)
    x_sample = nrm(ks[1], (DEC_BATCH, DEC_SEQ, D_MODEL), f32)
    cache_k = nrm(ks[2], (DEPTH, n_phys, PAGE_SIZE, N_HEADS, HEAD_DIM), f32)
    cache_v = nrm(ks[3], (DEPTH, n_phys, PAGE_SIZE, N_HEADS, HEAD_DIM), f32)
    cache_logf = jax.nn.log_sigmoid(3.0 + nrm(ks[4], (DEPTH, n_phys, PAGE_SIZE, N_HEADS), f32))
    page_table = jax.random.permutation(ks[5], n_phys)[:n_used].reshape(DEC_BATCH, n_pages).astype(jnp.int32)
    g_norm = 1.0 + 0.02 * nrm(ks[6], (DEPTH, D_MODEL), f32)
    w_in = nrm(ks[7], (DEPTH, D_MODEL, D_IN), f32) * D_MODEL ** -0.5
    b_f = 3.0 + 0.5 * nrm(ks[8], (DEPTH, N_HEADS), f32)
    g_q = 1.0 + 0.02 * nrm(ks[9], (DEPTH, HEAD_DIM), f32)
    g_k = 1.0 + 0.02 * nrm(ks[10], (DEPTH, HEAD_DIM), f32)
    g_v = 1.0 + 0.02 * nrm(ks[11], (DEPTH, D_CHK), f32)
    w_s = nrm(ks[12], (DEPTH, N_GROUPS, CHUNK, CHUNK), f32) * CHUNK ** -0.5
    b_s = 1.0 + 0.1 * nrm(ks[13], (DEPTH, N_GROUPS, CHUNK), f32)
    w_out = nrm(ks[14], (DEPTH, D_MIX, D_MODEL), f32) * D_MIX ** -0.5
    return {'x_prompt': x_prompt, 'x_sample': x_sample, 'cache_k': cache_k, 'cache_v': cache_v,
            'cache_logf': cache_logf, 'page_table': page_table, 'g_norm': g_norm, 'w_in': w_in,
            'b_f': b_f, 'g_q': g_q, 'g_k': g_k, 'g_v': g_v, 'w_s': w_s, 'b_s': b_s, 'w_out': w_out}


def reference(x_prompt, x_sample, cache_k, cache_v, cache_logf, page_table, g_norm, w_in,
              b_f, g_q, g_k, g_v, w_s, b_s, w_out):
    DB = x_sample.shape[0]
    P = page_table.shape[1] * PAGE_SIZE
    xp, xs = x_prompt, x_sample
    kp_l, vp_l, lp_l, ks_l, vs_l, ls_l, gs_l = [], [], [], [], [], [], []
    for l in range(DEPTH):
        q, k, v, logf, za, u, gv, zc = branch_inputs(xp, g_norm[l], w_in[l], b_f[l], g_q[l], g_k[l], g_v[l])
        attn = fox_prompt(q, k, v, logf)
        mix = chunk_mix(gv, w_s[l], b_s[l])
        xp = merge(xp, attn, za, mix, u, zc, w_out[l])
        kp_l.append(k); vp_l.append(v); lp_l.append(logf)
        q2, k2, v2, logf2, za2, u2, gv2, zc2 = branch_inputs(xs, g_norm[l], w_in[l], b_f[l], g_q[l], g_k[l], g_v[l])
        k_past = cache_k[l][page_table].reshape(DB, P, N_HEADS, HEAD_DIM)
        v_past = cache_v[l][page_table].reshape(DB, P, N_HEADS, HEAD_DIM)
        logf_past = cache_logf[l][page_table].reshape(DB, P, N_HEADS)
        attn2 = fox_sample(q2, k2, v2, logf2, k_past, v_past, logf_past)
        mix2 = chunk_mix(gv2, w_s[l], b_s[l])
        xs = merge(xs, attn2, za2, mix2, u2, zc2, w_out[l])
        ks_l.append(k2); vs_l.append(v2); ls_l.append(logf2); gs_l.append(gv2)
    k_prompt = jnp.stack(kp_l); v_prompt = jnp.stack(vp_l); logf_prompt = jnp.stack(lp_l)
    k_sample = jnp.stack(ks_l); v_sample = jnp.stack(vs_l); logf_sample = jnp.stack(ls_l)
    gv_sample = jnp.stack(gs_l)
    return (xp, xs, k_prompt, v_prompt, logf_prompt, k_sample, v_sample, logf_sample, gv_sample)
```

```python
import functools

import numpy as np
import jax
import jax.numpy as jnp
from jax import lax
from jax.experimental import pallas as pl
from jax.experimental.pallas import tpu as pltpu

F32 = jnp.float32
BF16 = jnp.bfloat16

D_MODEL = 1024
N_HEADS = 8
HEAD_DIM = 64
D_ATT = N_HEADS * HEAD_DIM
N_GROUPS = 4
CH_GROUP = 128
D_CHK = N_GROUPS * CH_GROUP
CHUNK = 128
PAGE = 128
EPS = 1e-6
SCALE = HEAD_DIM ** -0.5
LOG2E = 1.4426950408889634

LANES = 128
VMEM_LIMIT_BYTES = 56 * 1024 * 1024

TM = 512
TQ = 256
TK = 256
V_ROWS = 80
X_ROWS = 32
PP_ROWS = 256

_Q0, _K0, _V0, _F0 = 0, D_ATT, 2 * D_ATT, 3 * D_ATT
_ZA0 = 3 * D_ATT + N_HEADS
_U0 = _ZA0 + D_ATT
_GV0 = _U0 + D_CHK
_ZC0 = _GV0 + D_CHK
W_K, W_V, W_ZA, W_U, W_GV, W_ZC, W_F = (i * 512 for i in range(7))
W_COLS = W_F + LANES
WT_ROWS = 2 * D_ATT + 16


def _log_sigmoid(x):
    return jnp.minimum(x, 0.0) - jnp.log1p(jnp.exp(-jnp.abs(x)))


def _silu(x):
    return x / (1.0 + jnp.exp(-x))


def _rms_rows(x, gain):
    ms = jnp.sum(x * x, axis=-1, keepdims=True) * (1.0 / x.shape[-1])
    return x * lax.rsqrt(ms + EPS) * gain


def _rms_head_pairs(x, gain):
    outs = []
    for p in range(x.shape[-1] // LANES):
        xp = x[:, p * LANES:(p + 1) * LANES]
        sq = xp * xp
        lo = lax.broadcasted_iota(jnp.int32, xp.shape, 1) < HEAD_DIM
        s_lo = jnp.sum(jnp.where(lo, sq, 0.0), axis=-1, keepdims=True)
        s_hi = jnp.sum(jnp.where(lo, 0.0, sq), axis=-1, keepdims=True)
        r = jnp.where(lo, lax.rsqrt(s_lo * (1.0 / HEAD_DIM) + EPS),
                      lax.rsqrt(s_hi * (1.0 / HEAD_DIM) + EPS))
        outs.append(xp * r * gain[:, p * LANES:(p + 1) * LANES])
    return jnp.concatenate(outs, axis=-1)


def _rms_groups(x, gain):
    outs = []
    for g in range(N_GROUPS):
        xg = x[:, g * CH_GROUP:(g + 1) * CH_GROUP]
        outs.append(_rms_rows(xg, gain[:, g * CH_GROUP:(g + 1) * CH_GROUP]))
    return jnp.concatenate(outs, axis=-1)


def _split3(a):
    a1 = a.astype(BF16)
    r1 = a - a1.astype(F32)
    a2 = r1.astype(BF16)
    a3 = (r1 - a2.astype(F32)).astype(BF16)
    return a1, a2, a3


def _dot(a, b):
    return jnp.dot(a, b, preferred_element_type=F32)


def _dot_nt(a, b):
    return lax.dot_general(a, b, (((1,), (1,)), ((), ())), preferred_element_type=F32)


def _prompt_proj_kernel(x_ref, gn_ref, w_ref, wt_ref, bf_ref, bft_ref, gqt_ref, gk_ref, gv_ref,
                        ws_ref, bsb_ref, ltri_ref, utri_ref,
                        k_out, v_out, lf_out, kb_out, e_out, qt_out, ext_out, va_out, ga_out, m_out,
                        carry_n, carry_t):
    s_idx = pl.program_id(1)

    @pl.when(s_idx == 0)
    def _():
        carry_n[...] = jnp.zeros_like(carry_n)
        carry_t[...] = jnp.zeros_like(carry_t)

    x = x_ref[0]
    h = _rms_rows(x, gn_ref[...]).astype(BF16)

    k = _rms_head_pairs(_dot(h, w_ref[:, W_K:W_K + D_ATT]), gk_ref[...])
    k_out[0] = k
    kb_out[0] = k.astype(BF16)
    v_out[0] = _dot(h, w_ref[:, W_V:W_V + D_ATT])

    lf = _log_sigmoid(_dot(h, w_ref[:, W_F:W_F + LANES]) + bf_ref[...])
    lf_out[0] = lf[:, :N_HEADS]
    l1, l2, l3 = _split3(lf)
    ltri = ltri_ref[...]
    c = carry_n[...] + (_dot(ltri, l1) + _dot(ltri, l2) + _dot(ltri, l3))
    carry_n[...] = c[TM - 1:TM, :]
    c1, c2, c3 = _split3(c * LOG2E)
    lane = lax.broadcasted_iota(jnp.int32, c.shape, 1)
    e = jnp.where(lane < 8, -c1.astype(F32),
                  jnp.where(lane < 16, -c2.astype(F32),
                            jnp.where(lane < 24, -c3.astype(F32),
                                      jnp.where(lane < 27, 1.0, 0.0))))
    e_out[0] = e.astype(BF16)

    t = _dot_nt(wt_ref[...], h)
    q3 = t[0:D_ATT].reshape(N_HEADS, HEAD_DIM, TM)
    ssq = jnp.sum(q3 * q3, axis=1, keepdims=True)
    qn = (q3 * lax.rsqrt(ssq * (1.0 / HEAD_DIM) + EPS)).reshape(D_ATT, TM)
    qn = qn * gqt_ref[...] * (SCALE * LOG2E)
    qt_out[0] = qn.astype(BF16)

    vt = t[D_ATT:2 * D_ATT]
    row16 = lax.broadcasted_iota(jnp.int32, (V_ROWS - HEAD_DIM, TK), 0)
    tail = jnp.where(row16 == 0, 1.0, 0.0)
    for hh in range(N_HEADS):
        for kb in range(TM // TK):
            blk = vt[hh * HEAD_DIM:(hh + 1) * HEAD_DIM, kb * TK:(kb + 1) * TK]
            va_out[0, hh, kb] = jnp.concatenate([blk, tail], axis=0).astype(BF16)

    lft = _log_sigmoid(t[2 * D_ATT:2 * D_ATT + N_HEADS] + bft_ref[...])
    t1, t2, t3 = _split3(lft)
    utri = utri_ref[...]
    ct = carry_t[:, 0:1] + (_dot(t1, utri) + _dot(t2, utri) + _dot(t3, utri))
    carry_t[...] = jnp.broadcast_to(ct[:, TM - 1:TM], carry_t.shape)
    ct1, ct2, ct3 = _split3(ct * LOG2E)
    rid = lax.broadcasted_iota(jnp.int32, (X_ROWS, TM), 0)
    for hh in range(N_HEADS):
        sel = jnp.where(rid < 24, jnp.where((rid & 7) == hh, 1.0, 0.0), 0.0)
        ext = jnp.where(rid == 24, ct1[hh:hh + 1].astype(F32),
                        jnp.where(rid == 25, ct2[hh:hh + 1].astype(F32),
                                  jnp.where(rid == 26, ct3[hh:hh + 1].astype(F32), sel)))
        ext_out[0, hh] = ext.astype(BF16)

    ga_out[0] = _silu(_dot(h, w_ref[:, W_ZA:W_ZA + D_ATT])).astype(BF16)
    u = _dot(h, w_ref[:, W_U:W_U + D_CHK])
    gvn = _rms_groups(_dot(h, w_ref[:, W_GV:W_GV + D_CHK]), gv_ref[...]).astype(BF16)
    gate_c = _silu(_dot(h, w_ref[:, W_ZC:W_ZC + D_CHK]))
    tri = (lax.broadcasted_iota(jnp.int32, (CHUNK, CHUNK), 1)
           <= lax.broadcasted_iota(jnp.int32, (CHUNK, CHUNK), 0))
    for g in range(N_GROUPS):
        wg = jnp.where(tri, ws_ref[g], 0.0).astype(BF16)
        cs = slice(g * CH_GROUP, (g + 1) * CH_GROUP)
        for cc in range(TM // CHUNK):
            rs = slice(cc * CHUNK, (cc + 1) * CHUNK)
            mix = _dot(wg, gvn[rs, cs]) + bsb_ref[g]
            m_out[0, rs, cs] = (u[rs, cs] * mix * gate_c[rs, cs]).astype(BF16)


def _prompt_proj(x, gn, w, wt, bf128, bft, gqt, gk, gv, ws, bsb, ltri, utri):
    B, S, _ = x.shape
    nk = S // TK
    const2 = lambda shape: pl.BlockSpec(shape, lambda b, s: (0, 0))
    const3 = lambda shape: pl.BlockSpec(shape, lambda b, s: (0, 0, 0))
    row_blk = lambda width: pl.BlockSpec((1, TM, width), lambda b, s: (b, s, 0))
    out_shape = (
        jax.ShapeDtypeStruct((B, S, D_ATT), F32),
        jax.ShapeDtypeStruct((B, S, D_ATT), F32),
        jax.ShapeDtypeStruct((B, S, N_HEADS), F32),
        jax.ShapeDtypeStruct((B, S, D_ATT), BF16),
        jax.ShapeDtypeStruct((B, S, LANES), BF16),
        jax.ShapeDtypeStruct((B, D_ATT, S), BF16),
        jax.ShapeDtypeStruct((B, N_HEADS, X_ROWS, S), BF16),
        jax.ShapeDtypeStruct((B, N_HEADS, nk, V_ROWS, TK), BF16),
        jax.ShapeDtypeStruct((B, S, D_ATT), BF16),
        jax.ShapeDtypeStruct((B, S, D_CHK), BF16),
    )
    out_specs = (
        row_blk(D_ATT), row_blk(D_ATT), row_blk(N_HEADS), row_blk(D_ATT), row_blk(LANES),
        pl.BlockSpec((1, D_ATT, TM), lambda b, s: (b, 0, s)),
        pl.BlockSpec((1, N_HEADS, X_ROWS, TM), lambda b, s: (b, 0, 0, s)),
        pl.BlockSpec((1, N_HEADS, TM // TK, V_ROWS, TK), lambda b, s: (b, 0, s, 0, 0)),
        row_blk(D_ATT), row_blk(D_CHK),
    )
    in_specs = [
        row_blk(D_MODEL),
        const2((1, D_MODEL)),
        const2((D_MODEL, W_COLS)),
        const2((WT_ROWS, D_MODEL)),
        const2((1, LANES)),
        const2((N_HEADS, 1)),
        const2((D_ATT, 1)),
        const2((1, D_ATT)),
        const2((1, D_CHK)),
        const3((N_GROUPS, CHUNK, CHUNK)),
        const3((N_GROUPS, CHUNK, CH_GROUP)),
        const2((TM, TM)),
        const2((TM, TM)),
    ]
    return pl.pallas_call(
        _prompt_proj_kernel,
        out_shape=out_shape,
        grid=(B, S // TM),
        in_specs=in_specs,
        out_specs=out_specs,
        scratch_shapes=[pltpu.VMEM((1, LANES), F32), pltpu.VMEM((N_HEADS, LANES), F32)],
        compiler_params=pltpu.CompilerParams(
            dimension_semantics=("arbitrary", "arbitrary"), vmem_limit_bytes=VMEM_LIMIT_BYTES),
        name="prompt_proj",
    )(x, gn, w, wt, bf128, bft, gqt, gk, gv, ws, bsb, ltri, utri)


def _prompt_attn_kernel(qt_ref, ext_ref, kb_ref, e_ref, va_ref, o_ref, wq_ref):
    i = pl.program_id(2)

    wq_ref[...] = jnp.zeros_like(wq_ref)
    wq_ref[0:HEAD_DIM, 0:TQ] = qt_ref[0, 0:HEAD_DIM, :]
    wq_ref[HEAD_DIM:2 * HEAD_DIM, TQ:2 * TQ] = qt_ref[0, HEAD_DIM:2 * HEAD_DIM, :]
    wq_ref[LANES:LANES + X_ROWS, 0:TQ] = ext_ref[0, 0]
    wq_ref[LANES:LANES + X_ROWS, TQ:2 * TQ] = ext_ref[0, 1]

    def step(j, carry, masked):
        m, acc0, acc1 = carry
        off = pl.multiple_of(j * TK, TK)
        ke = jnp.concatenate([kb_ref[0, pl.ds(off, TK), :], e_ref[0, pl.ds(off, TK), :]], axis=1)
        s = _dot(ke, wq_ref[...])
        if masked:
            krow = lax.broadcasted_iota(jnp.int32, (TK, 2 * TQ), 0)
            qcol = lax.broadcasted_iota(jnp.int32, (TK, 2 * TQ), 1) & (TQ - 1)
            s = jnp.where(krow <= qcol, s, -jnp.inf)
        m_new = jnp.maximum(m, jnp.max(s, axis=0, keepdims=True))
        alpha = jnp.exp2(m - m_new)
        p = jnp.exp2(s - m_new).astype(BF16)
        acc0 = alpha[:, 0:TQ] * acc0 + _dot(va_ref[0, 0, j], p[:, 0:TQ])
        acc1 = alpha[:, TQ:2 * TQ] * acc1 + _dot(va_ref[0, 1, j], p[:, TQ:2 * TQ])
        return m_new, acc0, acc1

    init = (jnp.full((1, 2 * TQ), -jnp.inf, F32),
            jnp.zeros((V_ROWS, TQ), F32), jnp.zeros((V_ROWS, TQ), F32))
    carry = step(i, init, True)
    _, acc0, acc1 = lax.fori_loop(0, i, lambda j, cr: step(j, cr, False), carry)

    o0 = acc0[0:HEAD_DIM] / acc0[HEAD_DIM:HEAD_DIM + 1]
    o1 = acc1[0:HEAD_DIM] / acc1[HEAD_DIM:HEAD_DIM + 1]
    o_ref[0] = jnp.concatenate([o0, o1], axis=0).T.astype(BF16)


def _prompt_attn(qt, ext, kb, e, va):
    B, _, S = qt.shape
    nk = S // TK
    return pl.pallas_call(
        _prompt_attn_kernel,
        out_shape=jax.ShapeDtypeStruct((B, S, D_ATT), BF16),
        grid=(B, N_HEADS // 2, S // TQ),
        in_specs=[
            pl.BlockSpec((1, LANES, TQ), lambda b, p, i: (b, p, i)),
            pl.BlockSpec((1, 2, X_ROWS, TQ), lambda b, p, i: (b, p, 0, i)),
            pl.BlockSpec((1, S, LANES), lambda b, p, i: (b, 0, p)),
            pl.BlockSpec((1, S, LANES), lambda b, p, i: (b, 0, 0)),
            pl.BlockSpec((1, 2, nk, V_ROWS, TK), lambda b, p, i: (b, p, 0, 0, 0)),
        ],
        out_specs=pl.BlockSpec((1, TQ, LANES), lambda b, p, i: (b, i, p)),
        scratch_shapes=[pltpu.VMEM((2 * LANES, 2 * TQ), BF16)],
        compiler_params=pltpu.CompilerParams(
            dimension_semantics=("arbitrary", "arbitrary", "arbitrary"),
            vmem_limit_bytes=VMEM_LIMIT_BYTES),
        name="prompt_attn",
    )(qt, ext, kb, e, va)


def _merge_kernel(x_ref, a_ref, ga_ref, m_ref, wo_ref, y_ref):
    a = (a_ref[0].astype(F32) * ga_ref[0].astype(F32)).astype(BF16)
    y_ref[0] = (x_ref[0] + _dot(a, wo_ref[0:D_ATT, :]) + _dot(m_ref[0], wo_ref[D_ATT:D_ATT + D_CHK, :]))


def _merge(x, a, ga, m, wo, tm):
    B, S, _ = x.shape
    row_blk = lambda width: pl.BlockSpec((1, tm, width), lambda b, s: (b, s, 0))
    return pl.pallas_call(
        _merge_kernel,
        out_shape=jax.ShapeDtypeStruct((B, S, D_MODEL), F32),
        grid=(B, S // tm),
        in_specs=[row_blk(D_MODEL), row_blk(D_ATT), row_blk(D_ATT), row_blk(D_CHK),
                  pl.BlockSpec((D_ATT + D_CHK, D_MODEL), lambda b, s: (0, 0))],
        out_specs=row_blk(D_MODEL),
        compiler_params=pltpu.CompilerParams(
            dimension_semantics=("arbitrary", "arbitrary"), vmem_limit_bytes=VMEM_LIMIT_BYTES),
        name="merge",
    )(x, a, ga, m, wo)


def _sample_proj_kernel(x_ref, gn_ref, w_ref, wq_ref, bf_ref, gq_ref, gk_ref, gv_ref, w00_ref, b0_ref,
                        k_out, v_out, lf_out, gvn_out, q_out, ga_out, m_out):
    h = _rms_rows(x_ref[...], gn_ref[...]).astype(BF16)
    q_out[...] = _rms_head_pairs(_dot(h, wq_ref[...]), gq_ref[...]) * SCALE
    k_out[...] = _rms_head_pairs(_dot(h, w_ref[:, W_K:W_K + D_ATT]), gk_ref[...])
    v_out[...] = _dot(h, w_ref[:, W_V:W_V + D_ATT])
    lf = _log_sigmoid(_dot(h, w_ref[:, W_F:W_F + LANES]) + bf_ref[...])
    lf_out[...] = lf[:, :N_HEADS]
    ga_out[...] = _silu(_dot(h, w_ref[:, W_ZA:W_ZA + D_ATT])).astype(BF16)
    u = _dot(h, w_ref[:, W_U:W_U + D_CHK])
    gvn = _rms_groups(_dot(h, w_ref[:, W_GV:W_GV + D_CHK]), gv_ref[...])
    gvn_out[...] = gvn
    gate_c = _silu(_dot(h, w_ref[:, W_ZC:W_ZC + D_CHK]))
    m_out[...] = (u * (w00_ref[...] * gvn + b0_ref[...]) * gate_c).astype(BF16)


def _sample_proj(x, gn, w, wq, bf128, gq, gk, gv, w00, b0):
    n = x.shape[0]
    full = lambda a: pl.BlockSpec(a.shape, lambda i: (0,) * a.ndim)
    args = (x, gn, w, wq, bf128, gq, gk, gv, w00, b0)
    out_shape = (
        jax.ShapeDtypeStruct((n, D_ATT), F32), jax.ShapeDtypeStruct((n, D_ATT), F32),
        jax.ShapeDtypeStruct((n, N_HEADS), F32), jax.ShapeDtypeStruct((n, D_CHK), F32),
        jax.ShapeDtypeStruct((n, D_ATT), F32), jax.ShapeDtypeStruct((n, D_ATT), BF16),
        jax.ShapeDtypeStruct((n, D_CHK), BF16),
    )
    return pl.pallas_call(
        _sample_proj_kernel,
        out_shape=out_shape,
        grid=(1,),
        in_specs=[full(a) for a in args],
        out_specs=tuple(pl.BlockSpec(o.shape, lambda i: (0, 0)) for o in out_shape),
        compiler_params=pltpu.CompilerParams(
            dimension_semantics=("arbitrary",), vmem_limit_bytes=VMEM_LIMIT_BYTES),
        name="sample_proj",
    )(*args)


def _page_suffix_matrix():
    src = np.arange(PAGE * N_HEADS)
    src_pos, src_h = src // N_HEADS, src % N_HEADS
    dst = np.arange(N_HEADS * PAGE)
    dst_h, dst_pos = dst // PAGE, dst % PAGE
    same = src_h[:, None] == dst_h[None, :]
    within = same & (src_pos[:, None] > dst_pos[None, :])
    return np.concatenate([within, same], axis=1).astype(np.float32)


def _page_suffix_kernel(lf_ref, mat_ref, o_ref):
    a1, a2, a3 = _split3(lf_ref[...])
    mat = mat_ref[...]
    o_ref[...] = _dot(a1, mat) + _dot(a2, mat) + _dot(a3, mat)


def _page_suffix(lf2d, mat):
    n = lf2d.shape[0]
    width = PAGE * N_HEADS
    return pl.pallas_call(
        _page_suffix_kernel,
        out_shape=jax.ShapeDtypeStruct((n, 2 * width), F32),
        grid=(n // PP_ROWS,),
        in_specs=[pl.BlockSpec((PP_ROWS, width), lambda i: (i, 0)),
                  pl.BlockSpec((width, 2 * width), lambda i: (0, 0))],
        out_specs=pl.BlockSpec((PP_ROWS, 2 * width), lambda i: (i, 0)),
        compiler_params=pltpu.CompilerParams(
            dimension_semantics=("arbitrary",), vmem_limit_bytes=VMEM_LIMIT_BYTES),
        name="page_suffix",
    )(lf2d, mat)


def _sample_attn_kernel(n_pages, pt_ref, q_ref, kn_ref, vn_ref, lfn_ref, *refs):
    del pt_ref
    k_refs = refs[0:n_pages]
    v_refs = refs[n_pages:2 * n_pages]
    pp_refs = refs[2 * n_pages:3 * n_pages]
    o_ref = refs[3 * n_pages]

    row = lax.broadcasted_iota(jnp.int32, (N_HEADS, D_ATT), 0)
    col_head = lax.broadcasted_iota(jnp.int32, (N_HEADS, D_ATT), 1) // HEAD_DIM
    own = row == col_head
    qbd = jnp.where(own, jnp.broadcast_to(q_ref[0], (N_HEADS, D_ATT)), 0.0)
    qbd16 = qbd.astype(BF16)

    off = jnp.broadcast_to(lfn_ref[0], (N_HEADS, PAGE))
    scores = [None] * n_pages
    for p in range(n_pages - 1, -1, -1):
        kp = k_refs[p][0].astype(BF16)
        scores[p] = _dot_nt(qbd16, kp) + pp_refs[p][0, 0] + off
        off = off + pp_refs[p][0, 1]
    s_self = jnp.sum(qbd * kn_ref[0], axis=-1, keepdims=True)

    m = s_self
    for p in range(n_pages):
        m = jnp.maximum(m, jnp.max(scores[p], axis=-1, keepdims=True))
    p_self = jnp.exp(s_self - m)
    l = p_self
    acc = p_self * vn_ref[0]
    for p in range(n_pages):
        pr = jnp.exp(scores[p] - m)
        l = l + jnp.sum(pr, axis=-1, keepdims=True)
        acc = acc + _dot(pr.astype(BF16), v_refs[p][0].astype(BF16))
    out = jnp.where(own, acc / l, 0.0)
    o_ref[0] = jnp.sum(out, axis=0, keepdims=True)


def _sample_attn(page_table, q, kn, vn, lfn, cache_k, cache_v, pp):
    nb, n_pages = page_table.shape
    kv_spec = lambda j: pl.BlockSpec((1, PAGE, D_ATT), lambda b, pt: (pt[b, j], 0, 0))
    pp_spec = lambda j: pl.BlockSpec((1, 2, N_HEADS, PAGE), lambda b, pt: (pt[b, j], 0, 0, 0))
    tok = lambda width: pl.BlockSpec((1, 1, width), lambda b, pt: (b, 0, 0))
    in_specs = ([tok(D_ATT), tok(D_ATT), tok(D_ATT),
                 pl.BlockSpec((1, N_HEADS, 1), lambda b, pt: (b, 0, 0))]
                + [kv_spec(j) for j in range(n_pages)]
                + [kv_spec(j) for j in range(n_pages)]
                + [pp_spec(j) for j in range(n_pages)])
    grid_spec = pltpu.PrefetchScalarGridSpec(
        num_scalar_prefetch=1, grid=(nb,), in_specs=in_specs, out_specs=tok(D_ATT))
    return pl.pallas_call(
        functools.partial(_sample_attn_kernel, n_pages),
        out_shape=jax.ShapeDtypeStruct((nb, 1, D_ATT), F32),
        grid_spec=grid_spec,
        compiler_params=pltpu.CompilerParams(
            dimension_semantics=("arbitrary",), vmem_limit_bytes=VMEM_LIMIT_BYTES),
        name="sample_attn",
    )(page_table, q, kn, vn, lfn, *([cache_k] * n_pages), *([cache_v] * n_pages), *([pp] * n_pages))


def _layer(xp, xs, cache_k, cache_v, cache_logf, page_table, g_norm, w_in, b_f, g_q, g_k, g_v,
           w_s, b_s, w_out):
    B, S, _ = xp.shape
    nb = xs.shape[0]
    n_phys = cache_k.shape[0]

    cols = lambda a, n: w_in[:, a:a + n]
    wf = cols(_F0, N_HEADS)
    wf_pad = jnp.concatenate([wf, wf, wf, jnp.zeros((D_MODEL, LANES - 3 * N_HEADS), F32)], axis=1)
    w = jnp.concatenate([cols(_K0, D_ATT), cols(_V0, D_ATT), cols(_ZA0, D_ATT), cols(_U0, D_CHK),
                         cols(_GV0, D_CHK), cols(_ZC0, D_CHK), wf_pad], axis=1).astype(BF16)
    wq = cols(_Q0, D_ATT).astype(BF16)
    wt = jnp.concatenate([cols(_Q0, D_ATT), cols(_V0, D_ATT), wf, jnp.zeros((D_MODEL, 8), F32)],
                         axis=1).T.astype(BF16)
    wo = w_out.astype(BF16)
    gn = g_norm.reshape(1, D_MODEL)
    bf128 = jnp.concatenate([b_f, b_f, b_f, jnp.zeros((LANES - 3 * N_HEADS,), F32)]).reshape(1, LANES)
    bft = b_f.reshape(N_HEADS, 1)
    gq = jnp.tile(g_q, N_HEADS).reshape(1, D_ATT)
    gqt = gq.reshape(D_ATT, 1)
    gk = jnp.tile(g_k, N_HEADS).reshape(1, D_ATT)
    gv = g_v.reshape(1, D_CHK)
    bsb = jnp.broadcast_to(b_s[:, :, None], (N_GROUPS, CHUNK, CH_GROUP))
    w00 = jnp.repeat(w_s[:, 0, 0], CH_GROUP).reshape(1, D_CHK)
    b0 = jnp.repeat(b_s[:, 0], CH_GROUP).reshape(1, D_CHK)
    tri = np.tril(np.ones((TM, TM), np.float32))
    ltri = jnp.asarray(tri, BF16)
    utri = jnp.asarray(tri.T, BF16)

    k_p, v_p, lf_p, kb, e, qt, ext, va, ga, m = _prompt_proj(
        xp, gn, w, wt, bf128, bft, gqt, gk, gv, w_s, bsb, ltri, utri)
    attn = _prompt_attn(qt, ext, kb, e, va)
    y_p = _merge(xp, attn, ga, m, wo, TM)

    xs2 = xs.reshape(nb, D_MODEL)
    k_s, v_s, lf_s, gvn_s, q_s, ga_s, m_s = _sample_proj(xs2, gn, w, wq, bf128, gq, gk, gv, w00, b0)
    pp = _page_suffix(cache_logf.reshape(n_phys, PAGE * N_HEADS),
                      jnp.asarray(_page_suffix_matrix(), BF16))
    pp = pp.reshape(n_phys, 2, N_HEADS, PAGE)
    attn_s = _sample_attn(page_table, q_s.reshape(nb, 1, D_ATT), k_s.reshape(nb, 1, D_ATT),
                          v_s.reshape(nb, 1, D_ATT), lf_s.reshape(nb, N_HEADS, 1),
                          cache_k.reshape(n_phys, PAGE, D_ATT), cache_v.reshape(n_phys, PAGE, D_ATT), pp)
    y_s = _merge(xs2.reshape(1, nb, D_MODEL), attn_s.reshape(1, nb, D_ATT).astype(BF16),
                 ga_s.reshape(1, nb, D_ATT), m_s.reshape(1, nb, D_CHK), wo, nb)

    return (y_p, y_s.reshape(nb, 1, D_MODEL),
            k_p.reshape(B, S, N_HEADS, HEAD_DIM), v_p.reshape(B, S, N_HEADS, HEAD_DIM), lf_p,
            k_s.reshape(nb, 1, N_HEADS, HEAD_DIM), v_s.reshape(nb, 1, N_HEADS, HEAD_DIM),
            lf_s.reshape(nb, 1, N_HEADS), gvn_s.reshape(nb, 1, N_GROUPS, CH_GROUP))


def kernel(x_prompt, x_sample, cache_k, cache_v, cache_logf, page_table, g_norm, w_in, b_f, g_q, g_k,
           g_v, w_s, b_s, w_out):
    depth = w_in.shape[0]
    assert x_sample.shape[1] == 1, "one new token per decode sequence"
    xp, xs = x_prompt, x_sample
    outs = []
    for l in range(depth):
        res = _layer(xp, xs, cache_k[l], cache_v[l], cache_logf[l], page_table, g_norm[l], w_in[l],
                     b_f[l], g_q[l], g_k[l], g_v[l], w_s[l], b_s[l], w_out[l])
        xp, xs = res[0], res[1]
        outs.append(res[2:])
    stacked = tuple(jnp.stack([o[i] for o in outs]) for i in range(7))
    return (xp, xs) + stacked
```

```python
import functools

import numpy as np
import jax
import jax.numpy as jnp
from jax import lax
from jax.experimental import pallas as pl
from jax.experimental.pallas import tpu as pltpu

F32 = jnp.float32
BF16 = jnp.bfloat16

D_MODEL = 1024
N_HEADS = 8
HEAD_DIM = 64
D_ATT = N_HEADS * HEAD_DIM
N_GROUPS = 4
CH_GROUP = 128
D_CHK = N_GROUPS * CH_GROUP
CHUNK = 128
PAGE = 128
EPS = 1e-6
SCALE = HEAD_DIM ** -0.5
LOG2E = 1.4426950408889634

LANES = 128
VMEM_LIMIT_BYTES = 56 * 1024 * 1024

TM = 512
TQ = 256
TK = 256
V_ROWS = 80
X_ROWS = 32
PP_ROWS = 256

_Q0, _K0, _V0, _F0 = 0, D_ATT, 2 * D_ATT, 3 * D_ATT
_ZA0 = 3 * D_ATT + N_HEADS
_U0 = _ZA0 + D_ATT
_GV0 = _U0 + D_CHK
_ZC0 = _GV0 + D_CHK
W_K, W_V, W_ZA, W_U, W_GV, W_ZC, W_F = (i * 512 for i in range(7))
W_COLS = W_F + LANES
WT_ROWS = 2 * D_ATT + 16


def _log_sigmoid(x):
    return jnp.minimum(x, 0.0) - jnp.log1p(jnp.exp(-jnp.abs(x)))


def _silu(x):
    return x / (1.0 + jnp.exp(-x))


def _rms_rows(x, gain):
    ms = jnp.sum(x * x, axis=-1, keepdims=True) * (1.0 / x.shape[-1])
    return x * lax.rsqrt(ms + EPS) * gain


def _rms_head_pairs(x, gain):
    outs = []
    for p in range(x.shape[-1] // LANES):
        xp = x[:, p * LANES:(p + 1) * LANES]
        sq = xp * xp
        lo = lax.broadcasted_iota(jnp.int32, xp.shape, 1) < HEAD_DIM
        s_lo = jnp.sum(jnp.where(lo, sq, 0.0), axis=-1, keepdims=True)
        s_hi = jnp.sum(jnp.where(lo, 0.0, sq), axis=-1, keepdims=True)
        r = jnp.where(lo, lax.rsqrt(s_lo * (1.0 / HEAD_DIM) + EPS),
                      lax.rsqrt(s_hi * (1.0 / HEAD_DIM) + EPS))
        outs.append(xp * r * gain[:, p * LANES:(p + 1) * LANES])
    return jnp.concatenate(outs, axis=-1)


def _rms_groups(x, gain):
    outs = []
    for g in range(N_GROUPS):
        xg = x[:, g * CH_GROUP:(g + 1) * CH_GROUP]
        outs.append(_rms_rows(xg, gain[:, g * CH_GROUP:(g + 1) * CH_GROUP]))
    return jnp.concatenate(outs, axis=-1)


def _split3(a):
    a1 = a.astype(BF16)
    r1 = a - a1.astype(F32)
    a2 = r1.astype(BF16)
    a3 = (r1 - a2.astype(F32)).astype(BF16)
    return a1, a2, a3


def _dot(a, b):
    return jnp.dot(a, b, preferred_element_type=F32)


def _dot_nt(a, b):
    return lax.dot_general(a, b, (((1,), (1,)), ((), ())), preferred_element_type=F32)


def _prompt_proj_kernel(x_ref, gn_ref, w_ref, wt_ref, bf_ref, bft_ref, gqt_ref, gk_ref, gv_ref,
                        ws_ref, bsb_ref, ltri_ref, utri_ref,
                        k_out, v_out, lf_out, kb_out, e_out, qt_out, ext_out, va_out, ga_out, m_out,
                        carry_n, carry_t):
    s_idx = pl.program_id(1)

    @pl.when(s_idx == 0)
    def _():
        carry_n[...] = jnp.zeros_like(carry_n)
        carry_t[...] = jnp.zeros_like(carry_t)

    x = x_ref[0]
    h = _rms_rows(x, gn_ref[...]).astype(BF16)

    k = _rms_head_pairs(_dot(h, w_ref[:, W_K:W_K + D_ATT]), gk_ref[...])
    k_out[0] = k
    kb_out[0] = k.astype(BF16)
    v_out[0] = _dot(h, w_ref[:, W_V:W_V + D_ATT])

    lf = _log_sigmoid(_dot(h, w_ref[:, W_F:W_F + LANES]) + bf_ref[...])
    lf_out[0] = lf[:, :N_HEADS]
    l1, l2, l3 = _split3(lf)
    ltri = ltri_ref[...]
    c = carry_n[...] + (_dot(ltri, l1) + _dot(ltri, l2) + _dot(ltri, l3))
    carry_n[...] = c[TM - 1:TM, :]
    c1, c2, c3 = _split3(c * LOG2E)
    lane = lax.broadcasted_iota(jnp.int32, c.shape, 1)
    e = jnp.where(lane < 8, -c1.astype(F32),
                  jnp.where(lane < 16, -c2.astype(F32),
                            jnp.where(lane < 24, -c3.astype(F32),
                                      jnp.where(lane < 27, 1.0, 0.0))))
    e_out[0] = e.astype(BF16)

    t = _dot_nt(wt_ref[...], h)
    q3 = t[0:D_ATT].reshape(N_HEADS, HEAD_DIM, TM)
    ssq = jnp.sum(q3 * q3, axis=1, keepdims=True)
    qn = (q3 * lax.rsqrt(ssq * (1.0 / HEAD_DIM) + EPS)).reshape(D_ATT, TM)
    qn = qn * gqt_ref[...] * (SCALE * LOG2E)
    qt_out[0] = qn.astype(BF16)

    vt = t[D_ATT:2 * D_ATT]
    row16 = lax.broadcasted_iota(jnp.int32, (V_ROWS - HEAD_DIM, TK), 0)
    tail = jnp.where(row16 == 0, 1.0, 0.0)
    for hh in range(N_HEADS):
        for kb in range(TM // TK):
            blk = vt[hh * HEAD_DIM:(hh + 1) * HEAD_DIM, kb * TK:(kb + 1) * TK]
            va_out[0, hh, kb] = jnp.concatenate([blk, tail], axis=0).astype(BF16)

    lft = _log_sigmoid(t[2 * D_ATT:2 * D_ATT + N_HEADS] + bft_ref[...])
    t1, t2, t3 = _split3(lft)
    utri = utri_ref[...]
    ct = carry_t[:, 0:1] + (_dot(t1, utri) + _dot(t2, utri) + _dot(t3, utri))
    carry_t[...] = jnp.broadcast_to(ct[:, TM - 1:TM], carry_t.shape)
    ct1, ct2, ct3 = _split3(ct * LOG2E)
    rid = lax.broadcasted_iota(jnp.int32, (X_ROWS, TM), 0)
    for hh in range(N_HEADS):
        sel = jnp.where(rid < 24, jnp.where((rid & 7) == hh, 1.0, 0.0), 0.0)
        ext = jnp.where(rid == 24, ct1[hh:hh + 1].astype(F32),
                        jnp.where(rid == 25, ct2[hh:hh + 1].astype(F32),
                                  jnp.where(rid == 26, ct3[hh:hh + 1].astype(F32), sel)))
        ext_out[0, hh] = ext.astype(BF16)

    ga_out[0] = _silu(_dot(h, w_ref[:, W_ZA:W_ZA + D_ATT])).astype(BF16)
    u = _dot(h, w_ref[:, W_U:W_U + D_CHK])
    gvn = _rms_groups(_dot(h, w_ref[:, W_GV:W_GV + D_CHK]), gv_ref[...]).astype(BF16)
    gate_c = _silu(_dot(h, w_ref[:, W_ZC:W_ZC + D_CHK]))
    tri = (lax.broadcasted_iota(jnp.int32, (CHUNK, CHUNK), 1)
           <= lax.broadcasted_iota(jnp.int32, (CHUNK, CHUNK), 0))
    for g in range(N_GROUPS):
        wg = jnp.where(tri, ws_ref[g], 0.0).astype(BF16)
        cs = slice(g * CH_GROUP, (g + 1) * CH_GROUP)
        for cc in range(TM // CHUNK):
            rs = slice(cc * CHUNK, (cc + 1) * CHUNK)
            mix = _dot(wg, gvn[rs, cs]) + bsb_ref[g]
            m_out[0, rs, cs] = (u[rs, cs] * mix * gate_c[rs, cs]).astype(BF16)


def _prompt_proj(x, gn, w, wt, bf128, bft, gqt, gk, gv, ws, bsb, ltri, utri):
    B, S, _ = x.shape
    nk = S // TK
    const2 = lambda shape: pl.BlockSpec(shape, lambda b, s: (0, 0))
    const3 = lambda shape: pl.BlockSpec(shape, lambda b, s: (0, 0, 0))
    row_blk = lambda width: pl.BlockSpec((1, TM, width), lambda b, s: (b, s, 0))
    out_shape = (
        jax.ShapeDtypeStruct((B, S, D_ATT), F32),
        jax.ShapeDtypeStruct((B, S, D_ATT), F32),
        jax.ShapeDtypeStruct((B, S, N_HEADS), F32),
        jax.ShapeDtypeStruct((B, S, D_ATT), BF16),
        jax.ShapeDtypeStruct((B, S, LANES), BF16),
        jax.ShapeDtypeStruct((B, D_ATT, S), BF16),
        jax.ShapeDtypeStruct((B, N_HEADS, X_ROWS, S), BF16),
        jax.ShapeDtypeStruct((B, N_HEADS, nk, V_ROWS, TK), BF16),
        jax.ShapeDtypeStruct((B, S, D_ATT), BF16),
        jax.ShapeDtypeStruct((B, S, D_CHK), BF16),
    )
    out_specs = (
        row_blk(D_ATT), row_blk(D_ATT), row_blk(N_HEADS), row_blk(D_ATT), row_blk(LANES),
        pl.BlockSpec((1, D_ATT, TM), lambda b, s: (b, 0, s)),
        pl.BlockSpec((1, N_HEADS, X_ROWS, TM), lambda b, s: (b, 0, 0, s)),
        pl.BlockSpec((1, N_HEADS, TM // TK, V_ROWS, TK), lambda b, s: (b, 0, s, 0, 0)),
        row_blk(D_ATT), row_blk(D_CHK),
    )
    in_specs = [
        row_blk(D_MODEL),
        const2((1, D_MODEL)),
        const2((D_MODEL, W_COLS)),
        const2((WT_ROWS, D_MODEL)),
        const2((1, LANES)),
        const2((N_HEADS, 1)),
        const2((D_ATT, 1)),
        const2((1, D_ATT)),
        const2((1, D_CHK)),
        const3((N_GROUPS, CHUNK, CHUNK)),
        const3((N_GROUPS, CHUNK, CH_GROUP)),
        const2((TM, TM)),
        const2((TM, TM)),
    ]
    return pl.pallas_call(
        _prompt_proj_kernel,
        out_shape=out_shape,
        grid=(B, S // TM),
        in_specs=in_specs,
        out_specs=out_specs,
        scratch_shapes=[pltpu.VMEM((1, LANES), F32), pltpu.VMEM((N_HEADS, LANES), F32)],
        compiler_params=pltpu.CompilerParams(
            dimension_semantics=("arbitrary", "arbitrary"), vmem_limit_bytes=VMEM_LIMIT_BYTES),
        name="prompt_proj",
    )(x, gn, w, wt, bf128, bft, gqt, gk, gv, ws, bsb, ltri, utri)


def _prompt_attn_kernel(qt_ref, ext_ref, kb_ref, e_ref, va_ref, o_ref, wq_ref, s_ref):
    i = pl.program_id(2)
    n_full = i // 2

    wq_ref[...] = jnp.zeros_like(wq_ref)
    wq_ref[0:HEAD_DIM, 0:TQ] = qt_ref[0, 0:HEAD_DIM, :]
    wq_ref[HEAD_DIM:2 * HEAD_DIM, TQ:2 * TQ] = qt_ref[0, HEAD_DIM:2 * HEAD_DIM, :]
    wq_ref[LANES:LANES + X_ROWS, 0:TQ] = ext_ref[0, 0]
    wq_ref[LANES:LANES + X_ROWS, TQ:2 * TQ] = ext_ref[0, 1]

    def scores(sb):
        off = pl.multiple_of(sb * (2 * TK), 2 * TK)
        ke = jnp.concatenate([kb_ref[0, pl.ds(off, 2 * TK), :], e_ref[0, pl.ds(off, 2 * TK), :]], axis=1)
        s = _dot(ke, wq_ref[...])
        s_ref[...] = s
        return jnp.max(s, axis=0, keepdims=True)

    def accumulate(sb, p, alpha, acc0, acc1):
        p = p.astype(BF16)
        acc0 = (alpha[:, 0:TQ] * acc0 + _dot(va_ref[0, 0, 2 * sb], p[0:TK, 0:TQ])
                + _dot(va_ref[0, 0, 2 * sb + 1], p[TK:2 * TK, 0:TQ]))
        acc1 = (alpha[:, TQ:2 * TQ] * acc1 + _dot(va_ref[0, 1, 2 * sb], p[0:TK, TQ:2 * TQ])
                + _dot(va_ref[0, 1, 2 * sb + 1], p[TK:2 * TK, TQ:2 * TQ]))
        return acc0, acc1

    def body(sb, carry):
        m, acc0, acc1, blk_max = carry
        m_new = jnp.maximum(m, blk_max)
        alpha = jnp.exp2(m - m_new)
        p = jnp.exp2(s_ref[...] - m_new)
        acc0, acc1 = accumulate(sb, p, alpha, acc0, acc1)
        return m_new, acc0, acc1, scores(sb + 1)

    init = (jnp.full((1, 2 * TQ), -jnp.inf, F32),
            jnp.zeros((V_ROWS, TQ), F32), jnp.zeros((V_ROWS, TQ), F32), scores(0))
    m, acc0, acc1, _ = lax.fori_loop(0, n_full, body, init)

    krow = lax.broadcasted_iota(jnp.int32, (2 * TK, 2 * TQ), 0) + (n_full * (2 * TK) - i * TQ)
    qcol = lax.broadcasted_iota(jnp.int32, (2 * TK, 2 * TQ), 1) & (TQ - 1)
    s = jnp.where(krow <= qcol, s_ref[...], -jnp.inf)
    m_new = jnp.maximum(m, jnp.max(s, axis=0, keepdims=True))
    acc0, acc1 = accumulate(n_full, jnp.exp2(s - m_new), jnp.exp2(m - m_new), acc0, acc1)

    o0 = acc0[0:HEAD_DIM] / acc0[HEAD_DIM:HEAD_DIM + 1]
    o1 = acc1[0:HEAD_DIM] / acc1[HEAD_DIM:HEAD_DIM + 1]
    o_ref[0] = jnp.concatenate([o0, o1], axis=0).T.astype(BF16)


def _prompt_attn(qt, ext, kb, e, va):
    B, _, S = qt.shape
    nk = S // TK
    return pl.pallas_call(
        _prompt_attn_kernel,
        out_shape=jax.ShapeDtypeStruct((B, S, D_ATT), BF16),
        grid=(B, N_HEADS // 2, S // TQ),
        in_specs=[
            pl.BlockSpec((1, LANES, TQ), lambda b, p, i: (b, p, i)),
            pl.BlockSpec((1, 2, X_ROWS, TQ), lambda b, p, i: (b, p, 0, i)),
            pl.BlockSpec((1, S, LANES), lambda b, p, i: (b, 0, p)),
            pl.BlockSpec((1, S, LANES), lambda b, p, i: (b, 0, 0)),
            pl.BlockSpec((1, 2, nk, V_ROWS, TK), lambda b, p, i: (b, p, 0, 0, 0)),
        ],
        out_specs=pl.BlockSpec((1, TQ, LANES), lambda b, p, i: (b, i, p)),
        scratch_shapes=[pltpu.VMEM((2 * LANES, 2 * TQ), BF16), pltpu.VMEM((2 * TK, 2 * TQ), F32)],
        compiler_params=pltpu.CompilerParams(
            dimension_semantics=("arbitrary", "arbitrary", "arbitrary"),
            vmem_limit_bytes=VMEM_LIMIT_BYTES),
        name="prompt_attn",
    )(qt, ext, kb, e, va)


def _merge_kernel(x_ref, a_ref, ga_ref, m_ref, wo_ref, y_ref):
    a = (a_ref[0].astype(F32) * ga_ref[0].astype(F32)).astype(BF16)
    y_ref[0] = (x_ref[0] + _dot(a, wo_ref[0:D_ATT, :]) + _dot(m_ref[0], wo_ref[D_ATT:D_ATT + D_CHK, :]))


def _merge(x, a, ga, m, wo, tm):
    B, S, _ = x.shape
    row_blk = lambda width: pl.BlockSpec((1, tm, width), lambda b, s: (b, s, 0))
    return pl.pallas_call(
        _merge_kernel,
        out_shape=jax.ShapeDtypeStruct((B, S, D_MODEL), F32),
        grid=(B, S // tm),
        in_specs=[row_blk(D_MODEL), row_blk(D_ATT), row_blk(D_ATT), row_blk(D_CHK),
                  pl.BlockSpec((D_ATT + D_CHK, D_MODEL), lambda b, s: (0, 0))],
        out_specs=row_blk(D_MODEL),
        compiler_params=pltpu.CompilerParams(
            dimension_semantics=("arbitrary", "arbitrary"), vmem_limit_bytes=VMEM_LIMIT_BYTES),
        name="merge",
    )(x, a, ga, m, wo)


def _sample_proj_kernel(x_ref, gn_ref, w_ref, wq_ref, bf_ref, gq_ref, gk_ref, gv_ref, w00_ref, b0_ref,
                        k_out, v_out, lf_out, gvn_out, q_out, ga_out, m_out):
    h = _rms_rows(x_ref[...], gn_ref[...]).astype(BF16)
    q_out[...] = _rms_head_pairs(_dot(h, wq_ref[...]), gq_ref[...]) * SCALE
    k_out[...] = _rms_head_pairs(_dot(h, w_ref[:, W_K:W_K + D_ATT]), gk_ref[...])
    v_out[...] = _dot(h, w_ref[:, W_V:W_V + D_ATT])
    lf = _log_sigmoid(_dot(h, w_ref[:, W_F:W_F + LANES]) + bf_ref[...])
    lf_out[...] = lf[:, :N_HEADS]
    ga_out[...] = _silu(_dot(h, w_ref[:, W_ZA:W_ZA + D_ATT])).astype(BF16)
    u = _dot(h, w_ref[:, W_U:W_U + D_CHK])
    gvn = _rms_groups(_dot(h, w_ref[:, W_GV:W_GV + D_CHK]), gv_ref[...])
    gvn_out[...] = gvn
    gate_c = _silu(_dot(h, w_ref[:, W_ZC:W_ZC + D_CHK]))
    m_out[...] = (u * (w00_ref[...] * gvn + b0_ref[...]) * gate_c).astype(BF16)


def _sample_proj(x, gn, w, wq, bf128, gq, gk, gv, w00, b0):
    n = x.shape[0]
    full = lambda a: pl.BlockSpec(a.shape, lambda i: (0,) * a.ndim)
    args = (x, gn, w, wq, bf128, gq, gk, gv, w00, b0)
    out_shape = (
        jax.ShapeDtypeStruct((n, D_ATT), F32), jax.ShapeDtypeStruct((n, D_ATT), F32),
        jax.ShapeDtypeStruct((n, N_HEADS), F32), jax.ShapeDtypeStruct((n, D_CHK), F32),
        jax.ShapeDtypeStruct((n, D_ATT), F32), jax.ShapeDtypeStruct((n, D_ATT), BF16),
        jax.ShapeDtypeStruct((n, D_CHK), BF16),
    )
    return pl.pallas_call(
        _sample_proj_kernel,
        out_shape=out_shape,
        grid=(1,),
        in_specs=[full(a) for a in args],
        out_specs=tuple(pl.BlockSpec(o.shape, lambda i: (0, 0)) for o in out_shape),
        compiler_params=pltpu.CompilerParams(
            dimension_semantics=("arbitrary",), vmem_limit_bytes=VMEM_LIMIT_BYTES),
        name="sample_proj",
    )(*args)


def _page_suffix_matrix():
    idx = np.arange(N_HEADS * PAGE)
    src_h, src_pos = idx // PAGE, idx % PAGE
    dst_h, dst_pos = idx // PAGE, idx % PAGE
    same = src_h[:, None] == dst_h[None, :]
    within = same & (src_pos[:, None] > dst_pos[None, :])
    return np.concatenate([within, same], axis=1).astype(np.float32)


def _page_suffix_kernel(lf_ref, mat_ref, o_ref):
    a1, a2, a3 = _split3(lf_ref[...])
    mat = mat_ref[...]
    o_ref[...] = _dot(a1, mat) + _dot(a2, mat) + _dot(a3, mat)


def _page_suffix(lf2d, mat):
    n = lf2d.shape[0]
    width = PAGE * N_HEADS
    return pl.pallas_call(
        _page_suffix_kernel,
        out_shape=jax.ShapeDtypeStruct((n, 2 * width), F32),
        grid=(n // PP_ROWS,),
        in_specs=[pl.BlockSpec((PP_ROWS, width), lambda i: (i, 0)),
                  pl.BlockSpec((width, 2 * width), lambda i: (0, 0))],
        out_specs=pl.BlockSpec((PP_ROWS, 2 * width), lambda i: (i, 0)),
        compiler_params=pltpu.CompilerParams(
            dimension_semantics=("arbitrary",), vmem_limit_bytes=VMEM_LIMIT_BYTES),
        name="page_suffix",
    )(lf2d, mat)


def _sample_attn_kernel(n_pages, pt_ref, q_ref, kn_ref, vn_ref, lfn_ref, *refs):
    del pt_ref
    k_refs = refs[0:n_pages]
    v_refs = refs[n_pages:2 * n_pages]
    pp_refs = refs[2 * n_pages:3 * n_pages]
    o_ref = refs[3 * n_pages]

    row = lax.broadcasted_iota(jnp.int32, (N_HEADS, D_ATT), 0)
    col_head = lax.broadcasted_iota(jnp.int32, (N_HEADS, D_ATT), 1) // HEAD_DIM
    own = row == col_head
    qbd = jnp.where(own, jnp.broadcast_to(q_ref[0], (N_HEADS, D_ATT)), 0.0)
    qbd16 = qbd.astype(BF16)

    off = jnp.broadcast_to(lfn_ref[0], (N_HEADS, PAGE))
    scores = [None] * n_pages
    for p in range(n_pages - 1, -1, -1):
        kp = k_refs[p][0].astype(BF16)
        scores[p] = _dot(qbd16, kp) + pp_refs[p][0, 0] + off
        off = off + pp_refs[p][0, 1]
    s_self = jnp.sum(qbd * kn_ref[0], axis=-1, keepdims=True)

    m = s_self
    for p in range(n_pages):
        m = jnp.maximum(m, jnp.max(scores[p], axis=-1, keepdims=True))
    p_self = jnp.exp(s_self - m)
    l = p_self
    acc = p_self * vn_ref[0]
    for p in range(n_pages):
        pr = jnp.exp(scores[p] - m)
        l = l + jnp.sum(pr, axis=-1, keepdims=True)
        acc = acc + _dot_nt(pr.astype(BF16), v_refs[p][0].astype(BF16))
    out = jnp.where(own, acc / l, 0.0)
    o_ref[0] = jnp.sum(out, axis=0, keepdims=True)


def _sample_attn(page_table, q, kn, vn, lfn, cache_k, cache_v, pp):
    nb, n_pages = page_table.shape
    kv_spec = lambda j: pl.BlockSpec((1, D_ATT, PAGE), lambda b, pt: (pt[b, j], 0, 0))
    pp_spec = lambda j: pl.BlockSpec((1, 2, N_HEADS, PAGE), lambda b, pt: (pt[b, j], 0, 0, 0))
    tok = lambda width: pl.BlockSpec((1, 1, width), lambda b, pt: (b, 0, 0))
    in_specs = ([tok(D_ATT), tok(D_ATT), tok(D_ATT),
                 pl.BlockSpec((1, N_HEADS, 1), lambda b, pt: (b, 0, 0))]
                + [kv_spec(j) for j in range(n_pages)]
                + [kv_spec(j) for j in range(n_pages)]
                + [pp_spec(j) for j in range(n_pages)])
    grid_spec = pltpu.PrefetchScalarGridSpec(
        num_scalar_prefetch=1, grid=(nb,), in_specs=in_specs, out_specs=tok(D_ATT))
    return pl.pallas_call(
        functools.partial(_sample_attn_kernel, n_pages),
        out_shape=jax.ShapeDtypeStruct((nb, 1, D_ATT), F32),
        grid_spec=grid_spec,
        compiler_params=pltpu.CompilerParams(
            dimension_semantics=("arbitrary",), vmem_limit_bytes=VMEM_LIMIT_BYTES),
        name="sample_attn",
    )(page_table, q, kn, vn, lfn, *([cache_k] * n_pages), *([cache_v] * n_pages), *([pp] * n_pages))


def _layer(xp, xs, cache_k, cache_v, cache_logf, page_table, g_norm, w_in, b_f, g_q, g_k, g_v,
           w_s, b_s, w_out):
    B, S, _ = xp.shape
    nb = xs.shape[0]
    n_phys = cache_k.shape[0]

    cols = lambda a, n: w_in[:, a:a + n]
    wf = cols(_F0, N_HEADS)
    wf_pad = jnp.concatenate([wf, wf, wf, jnp.zeros((D_MODEL, LANES - 3 * N_HEADS), F32)], axis=1)
    w = jnp.concatenate([cols(_K0, D_ATT), cols(_V0, D_ATT), cols(_ZA0, D_ATT), cols(_U0, D_CHK),
                         cols(_GV0, D_CHK), cols(_ZC0, D_CHK), wf_pad], axis=1).astype(BF16)
    wq = cols(_Q0, D_ATT).astype(BF16)
    wt = jnp.concatenate([cols(_Q0, D_ATT), cols(_V0, D_ATT), wf, jnp.zeros((D_MODEL, 8), F32)],
                         axis=1).T.astype(BF16)
    wo = w_out.astype(BF16)
    gn = g_norm.reshape(1, D_MODEL)
    bf128 = jnp.concatenate([b_f, b_f, b_f, jnp.zeros((LANES - 3 * N_HEADS,), F32)]).reshape(1, LANES)
    bft = b_f.reshape(N_HEADS, 1)
    gq = jnp.tile(g_q, N_HEADS).reshape(1, D_ATT)
    gqt = gq.reshape(D_ATT, 1)
    gk = jnp.tile(g_k, N_HEADS).reshape(1, D_ATT)
    gv = g_v.reshape(1, D_CHK)
    bsb = jnp.broadcast_to(b_s[:, :, None], (N_GROUPS, CHUNK, CH_GROUP))
    w00 = jnp.repeat(w_s[:, 0, 0], CH_GROUP).reshape(1, D_CHK)
    b0 = jnp.repeat(b_s[:, 0], CH_GROUP).reshape(1, D_CHK)
    tri = np.tril(np.ones((TM, TM), np.float32))
    ltri = jnp.asarray(tri, BF16)
    utri = jnp.asarray(tri.T, BF16)

    k_p, v_p, lf_p, kb, e, qt, ext, va, ga, m = _prompt_proj(
        xp, gn, w, wt, bf128, bft, gqt, gk, gv, w_s, bsb, ltri, utri)
    attn = _prompt_attn(qt, ext, kb, e, va)
    y_p = _merge(xp, attn, ga, m, wo, TM)

    xs2 = xs.reshape(nb, D_MODEL)
    k_s, v_s, lf_s, gvn_s, q_s, ga_s, m_s = _sample_proj(xs2, gn, w, wq, bf128, gq, gk, gv, w00, b0)
    lf_pages = jnp.transpose(cache_logf, (0, 2, 1)).reshape(n_phys, N_HEADS * PAGE)
    kt_pages = jnp.transpose(cache_k, (0, 2, 3, 1)).reshape(n_phys, D_ATT, PAGE)
    vt_pages = jnp.transpose(cache_v, (0, 2, 3, 1)).reshape(n_phys, D_ATT, PAGE)
    pp = _page_suffix(lf_pages, jnp.asarray(_page_suffix_matrix(), BF16))
    pp = pp.reshape(n_phys, 2, N_HEADS, PAGE)
    attn_s = _sample_attn(page_table, q_s.reshape(nb, 1, D_ATT), k_s.reshape(nb, 1, D_ATT),
                          v_s.reshape(nb, 1, D_ATT), lf_s.reshape(nb, N_HEADS, 1),
                          kt_pages, vt_pages, pp)
    y_s = _merge(xs2.reshape(1, nb, D_MODEL), attn_s.reshape(1, nb, D_ATT).astype(BF16),
                 ga_s.reshape(1, nb, D_ATT), m_s.reshape(1, nb, D_CHK), wo, nb)

    return (y_p, y_s.reshape(nb, 1, D_MODEL),
            k_p.reshape(B, S, N_HEADS, HEAD_DIM), v_p.reshape(B, S, N_HEADS, HEAD_DIM), lf_p,
            k_s.reshape(nb, 1, N_HEADS, HEAD_DIM), v_s.reshape(nb, 1, N_HEADS, HEAD_DIM),
            lf_s.reshape(nb, 1, N_HEADS), gvn_s.reshape(nb, 1, N_GROUPS, CH_GROUP))


def kernel(x_prompt, x_sample, cache_k, cache_v, cache_logf, page_table, g_norm, w_in, b_f, g_q, g_k,
           g_v, w_s, b_s, w_out):
    depth = w_in.shape[0]
    assert x_sample.shape[1] == 1, "one new token per decode sequence"
    xp, xs = x_prompt, x_sample
    outs = []
    for l in range(depth):
        res = _layer(xp, xs, cache_k[l], cache_v[l], cache_logf[l], page_table, g_norm[l], w_in[l],
                     b_f[l], g_q[l], g_k[l], g_v[l], w_s[l], b_s[l], w_out[l])
        xp, xs = res[0], res[1]
        outs.append(res[2:])
    stacked = tuple(jnp.stack([o[i] for o in outs]) for i in range(7))
    return (xp, xs) + stacked
```

```python
import functools

import numpy as np
import jax
import jax.numpy as jnp
from jax import lax
from jax.experimental import pallas as pl
from jax.experimental.pallas import tpu as pltpu

F32 = jnp.float32
BF16 = jnp.bfloat16

D_MODEL = 1024
N_HEADS = 8
HEAD_DIM = 64
D_ATT = N_HEADS * HEAD_DIM
N_GROUPS = 4
CH_GROUP = 128
D_CHK = N_GROUPS * CH_GROUP
CHUNK = 128
PAGE = 128
EPS = 1e-6
SCALE = HEAD_DIM ** -0.5
LOG2E = 1.4426950408889634

LANES = 128
VMEM_LIMIT_BYTES = 56 * 1024 * 1024

TM = 512
TQ = 512
TK = 256
KS = 512
V_ROWS = 80
X_ROWS = 32
PP_ROWS = 256

_Q0, _K0, _V0, _F0 = 0, D_ATT, 2 * D_ATT, 3 * D_ATT
_ZA0 = 3 * D_ATT + N_HEADS
_U0 = _ZA0 + D_ATT
_GV0 = _U0 + D_CHK
_ZC0 = _GV0 + D_CHK
W_K, W_V, W_ZA, W_U, W_GV, W_ZC, W_F = (i * 512 for i in range(7))
W_COLS = W_F + LANES
WT_ROWS = 2 * D_ATT + 16


def _log_sigmoid(x):
    return jnp.minimum(x, 0.0) - jnp.log1p(jnp.exp(-jnp.abs(x)))


def _silu(x):
    return x / (1.0 + jnp.exp(-x))


def _rms_rows(x, gain):
    ms = jnp.sum(x * x, axis=-1, keepdims=True) * (1.0 / x.shape[-1])
    return x * lax.rsqrt(ms + EPS) * gain


def _rms_head_pairs(x, gain):
    outs = []
    for p in range(x.shape[-1] // LANES):
        xp = x[:, p * LANES:(p + 1) * LANES]
        sq = xp * xp
        lo = lax.broadcasted_iota(jnp.int32, xp.shape, 1) < HEAD_DIM
        s_lo = jnp.sum(jnp.where(lo, sq, 0.0), axis=-1, keepdims=True)
        s_hi = jnp.sum(jnp.where(lo, 0.0, sq), axis=-1, keepdims=True)
        r = jnp.where(lo, lax.rsqrt(s_lo * (1.0 / HEAD_DIM) + EPS),
                      lax.rsqrt(s_hi * (1.0 / HEAD_DIM) + EPS))
        outs.append(xp * r * gain[:, p * LANES:(p + 1) * LANES])
    return jnp.concatenate(outs, axis=-1)


def _rms_groups(x, gain):
    outs = []
    for g in range(N_GROUPS):
        xg = x[:, g * CH_GROUP:(g + 1) * CH_GROUP]
        outs.append(_rms_rows(xg, gain[:, g * CH_GROUP:(g + 1) * CH_GROUP]))
    return jnp.concatenate(outs, axis=-1)


def _split3(a):
    a1 = a.astype(BF16)
    r1 = a - a1.astype(F32)
    a2 = r1.astype(BF16)
    a3 = (r1 - a2.astype(F32)).astype(BF16)
    return a1, a2, a3


def _dot(a, b):
    return jnp.dot(a, b, preferred_element_type=F32)


def _dot_nt(a, b):
    return lax.dot_general(a, b, (((1,), (1,)), ((), ())), preferred_element_type=F32)


def _prompt_proj_kernel(x_ref, gn_ref, w_ref, wt_ref, bf_ref, bft_ref, gqt_ref, gk_ref, gv_ref,
                        ws_ref, bsb_ref, ltri_ref, utri_ref,
                        k_out, v_out, lf_out, kb_out, e_out, qt_out, ext_out, va_out, ga_out, m_out,
                        carry_n, carry_t):
    s_idx = pl.program_id(1)

    @pl.when(s_idx == 0)
    def _():
        carry_n[...] = jnp.zeros_like(carry_n)
        carry_t[...] = jnp.zeros_like(carry_t)

    x = x_ref[0]
    h = _rms_rows(x, gn_ref[...]).astype(BF16)

    k = _rms_head_pairs(_dot(h, w_ref[:, W_K:W_K + D_ATT]), gk_ref[...])
    k_out[0] = k
    kb_out[0] = k.astype(BF16)
    v_out[0] = _dot(h, w_ref[:, W_V:W_V + D_ATT])

    lf = _log_sigmoid(_dot(h, w_ref[:, W_F:W_F + LANES]) + bf_ref[...])
    lf_out[0] = lf[:, :N_HEADS]
    l1, l2, l3 = _split3(lf)
    ltri = ltri_ref[...]
    c = carry_n[...] + (_dot(ltri, l1) + _dot(ltri, l2) + _dot(ltri, l3))
    carry_n[...] = c[TM - 1:TM, :]
    c1, c2, c3 = _split3(c * LOG2E)
    lane = lax.broadcasted_iota(jnp.int32, c.shape, 1)
    e = jnp.where(lane < 8, -c1.astype(F32),
                  jnp.where(lane < 16, -c2.astype(F32),
                            jnp.where(lane < 24, -c3.astype(F32),
                                      jnp.where(lane < 27, 1.0, 0.0))))
    e_out[0] = e.astype(BF16)

    t = _dot_nt(wt_ref[...], h)
    q3 = t[0:D_ATT].reshape(N_HEADS, HEAD_DIM, TM)
    ssq = jnp.sum(q3 * q3, axis=1, keepdims=True)
    qn = (q3 * lax.rsqrt(ssq * (1.0 / HEAD_DIM) + EPS)).reshape(D_ATT, TM)
    qn = qn * gqt_ref[...] * (SCALE * LOG2E)
    qt_out[0] = qn.astype(BF16)

    vt = t[D_ATT:2 * D_ATT]
    row16 = lax.broadcasted_iota(jnp.int32, (V_ROWS - HEAD_DIM, TK), 0)
    tail = jnp.where(row16 == 0, 1.0, 0.0)
    for hh in range(N_HEADS):
        for kb in range(TM // TK):
            blk = vt[hh * HEAD_DIM:(hh + 1) * HEAD_DIM, kb * TK:(kb + 1) * TK]
            va_out[0, hh, kb] = jnp.concatenate([blk, tail], axis=0).astype(BF16)

    lft = _log_sigmoid(t[2 * D_ATT:2 * D_ATT + N_HEADS] + bft_ref[...])
    t1, t2, t3 = _split3(lft)
    utri = utri_ref[...]
    ct = carry_t[:, 0:1] + (_dot(t1, utri) + _dot(t2, utri) + _dot(t3, utri))
    carry_t[...] = jnp.broadcast_to(ct[:, TM - 1:TM], carry_t.shape)
    ct1, ct2, ct3 = _split3(ct * LOG2E)
    rid = lax.broadcasted_iota(jnp.int32, (X_ROWS, TM), 0)
    for hh in range(N_HEADS):
        sel = jnp.where(rid < 24, jnp.where((rid & 7) == hh, 1.0, 0.0), 0.0)
        ext = jnp.where(rid == 24, ct1[hh:hh + 1].astype(F32),
                        jnp.where(rid == 25, ct2[hh:hh + 1].astype(F32),
                                  jnp.where(rid == 26, ct3[hh:hh + 1].astype(F32), sel)))
        ext_out[0, hh] = ext.astype(BF16)

    ga_out[0] = _silu(_dot(h, w_ref[:, W_ZA:W_ZA + D_ATT])).astype(BF16)
    u = _dot(h, w_ref[:, W_U:W_U + D_CHK])
    gvn = _rms_groups(_dot(h, w_ref[:, W_GV:W_GV + D_CHK]), gv_ref[...]).astype(BF16)
    gate_c = _silu(_dot(h, w_ref[:, W_ZC:W_ZC + D_CHK]))
    tri = (lax.broadcasted_iota(jnp.int32, (CHUNK, CHUNK), 1)
           <= lax.broadcasted_iota(jnp.int32, (CHUNK, CHUNK), 0))
    for g in range(N_GROUPS):
        wg = jnp.where(tri, ws_ref[g], 0.0).astype(BF16)
        cs = slice(g * CH_GROUP, (g + 1) * CH_GROUP)
        for cc in range(TM // CHUNK):
            rs = slice(cc * CHUNK, (cc + 1) * CHUNK)
            mix = _dot(wg, gvn[rs, cs]) + bsb_ref[g]
            m_out[0, rs, cs] = (u[rs, cs] * mix * gate_c[rs, cs]).astype(BF16)


def _prompt_proj(x, gn, w, wt, bf128, bft, gqt, gk, gv, ws, bsb, ltri, utri):
    B, S, _ = x.shape
    nk = S // TK
    const2 = lambda shape: pl.BlockSpec(shape, lambda b, s: (0, 0))
    const3 = lambda shape: pl.BlockSpec(shape, lambda b, s: (0, 0, 0))
    row_blk = lambda width: pl.BlockSpec((1, TM, width), lambda b, s: (b, s, 0))
    out_shape = (
        jax.ShapeDtypeStruct((B, S, D_ATT), F32),
        jax.ShapeDtypeStruct((B, S, D_ATT), F32),
        jax.ShapeDtypeStruct((B, S, N_HEADS), F32),
        jax.ShapeDtypeStruct((B, S, D_ATT), BF16),
        jax.ShapeDtypeStruct((B, S, LANES), BF16),
        jax.ShapeDtypeStruct((B, D_ATT, S), BF16),
        jax.ShapeDtypeStruct((B, N_HEADS, X_ROWS, S), BF16),
        jax.ShapeDtypeStruct((B, N_HEADS, nk, V_ROWS, TK), BF16),
        jax.ShapeDtypeStruct((B, S, D_ATT), BF16),
        jax.ShapeDtypeStruct((B, S, D_CHK), BF16),
    )
    out_specs = (
        row_blk(D_ATT), row_blk(D_ATT), row_blk(N_HEADS), row_blk(D_ATT), row_blk(LANES),
        pl.BlockSpec((1, D_ATT, TM), lambda b, s: (b, 0, s)),
        pl.BlockSpec((1, N_HEADS, X_ROWS, TM), lambda b, s: (b, 0, 0, s)),
        pl.BlockSpec((1, N_HEADS, TM // TK, V_ROWS, TK), lambda b, s: (b, 0, s, 0, 0)),
        row_blk(D_ATT), row_blk(D_CHK),
    )
    in_specs = [
        row_blk(D_MODEL),
        const2((1, D_MODEL)),
        const2((D_MODEL, W_COLS)),
        const2((WT_ROWS, D_MODEL)),
        const2((1, LANES)),
        const2((N_HEADS, 1)),
        const2((D_ATT, 1)),
        const2((1, D_ATT)),
        const2((1, D_CHK)),
        const3((N_GROUPS, CHUNK, CHUNK)),
        const3((N_GROUPS, CHUNK, CH_GROUP)),
        const2((TM, TM)),
        const2((TM, TM)),
    ]
    return pl.pallas_call(
        _prompt_proj_kernel,
        out_shape=out_shape,
        grid=(B, S // TM),
        in_specs=in_specs,
        out_specs=out_specs,
        scratch_shapes=[pltpu.VMEM((1, LANES), F32), pltpu.VMEM((N_HEADS, LANES), F32)],
        compiler_params=pltpu.CompilerParams(
            dimension_semantics=("arbitrary", "arbitrary"), vmem_limit_bytes=VMEM_LIMIT_BYTES),
        name="prompt_proj",
    )(x, gn, w, wt, bf128, bft, gqt, gk, gv, ws, bsb, ltri, utri)


def _prompt_attn_kernel(qt_ref, ext_ref, kb_ref, e_ref, va_ref, o_ref, wq_ref, s_ref):
    i = pl.program_id(2)
    n_full = (i * TQ) // KS

    wq_ref[...] = jnp.zeros_like(wq_ref)
    wq_ref[0:HEAD_DIM, 0:TQ] = qt_ref[0, 0:HEAD_DIM, :]
    wq_ref[HEAD_DIM:2 * HEAD_DIM, TQ:2 * TQ] = qt_ref[0, HEAD_DIM:2 * HEAD_DIM, :]
    wq_ref[LANES:LANES + X_ROWS, 0:TQ] = ext_ref[0, 0]
    wq_ref[LANES:LANES + X_ROWS, TQ:2 * TQ] = ext_ref[0, 1]

    def scores(sb):
        off = pl.multiple_of(sb * KS, KS)
        ke = jnp.concatenate([kb_ref[0, pl.ds(off, KS), :], e_ref[0, pl.ds(off, KS), :]], axis=1)
        s = _dot(ke, wq_ref[...])
        s_ref[...] = s
        return jnp.max(s, axis=0, keepdims=True)

    def accumulate(sb, p, alpha, accs):
        p = p.astype(BF16)
        out = []
        for hh in range(2):
            cols = slice(hh * TQ, (hh + 1) * TQ)
            acc = alpha[:, cols] * accs[hh]
            for kk in range(KS // TK):
                acc = acc + _dot(va_ref[0, hh, sb * (KS // TK) + kk], p[kk * TK:(kk + 1) * TK, cols])
            out.append(acc)
        return tuple(out)

    def body(sb, carry):
        m, accs, blk_max = carry
        m_new = jnp.maximum(m, blk_max)
        p = jnp.exp2(s_ref[...] - m_new)
        accs = accumulate(sb, p, jnp.exp2(m - m_new), accs)
        return m_new, accs, scores(sb + 1)

    zero = jnp.zeros((V_ROWS, TQ), F32)
    init = (jnp.full((1, 2 * TQ), -jnp.inf, F32), (zero, zero), scores(0))
    m, accs, _ = lax.fori_loop(0, n_full, body, init)

    krow = lax.broadcasted_iota(jnp.int32, (KS, 2 * TQ), 0) + (n_full * KS - i * TQ)
    qcol = lax.broadcasted_iota(jnp.int32, (KS, 2 * TQ), 1) & (TQ - 1)
    s = jnp.where(krow <= qcol, s_ref[...], -jnp.inf)
    m_new = jnp.maximum(m, jnp.max(s, axis=0, keepdims=True))
    acc0, acc1 = accumulate(n_full, jnp.exp2(s - m_new), jnp.exp2(m - m_new), accs)

    o0 = acc0[0:HEAD_DIM] / acc0[HEAD_DIM:HEAD_DIM + 1]
    o1 = acc1[0:HEAD_DIM] / acc1[HEAD_DIM:HEAD_DIM + 1]
    o_ref[0] = jnp.concatenate([o0, o1], axis=0).T.astype(BF16)


def _prompt_attn(qt, ext, kb, e, va):
    B, _, S = qt.shape
    nk = S // TK
    return pl.pallas_call(
        _prompt_attn_kernel,
        out_shape=jax.ShapeDtypeStruct((B, S, D_ATT), BF16),
        grid=(B, N_HEADS // 2, S // TQ),
        in_specs=[
            pl.BlockSpec((1, LANES, TQ), lambda b, p, i: (b, p, i)),
            pl.BlockSpec((1, 2, X_ROWS, TQ), lambda b, p, i: (b, p, 0, i)),
            pl.BlockSpec((1, S, LANES), lambda b, p, i: (b, 0, p)),
            pl.BlockSpec((1, S, LANES), lambda b, p, i: (b, 0, 0)),
            pl.BlockSpec((1, 2, nk, V_ROWS, TK), lambda b, p, i: (b, p, 0, 0, 0)),
        ],
        out_specs=pl.BlockSpec((1, TQ, LANES), lambda b, p, i: (b, i, p)),
        scratch_shapes=[pltpu.VMEM((2 * LANES, 2 * TQ), BF16), pltpu.VMEM((KS, 2 * TQ), F32)],
        compiler_params=pltpu.CompilerParams(
            dimension_semantics=("arbitrary", "arbitrary", "arbitrary"),
            vmem_limit_bytes=VMEM_LIMIT_BYTES),
        name="prompt_attn",
    )(qt, ext, kb, e, va)


def _merge_kernel(x_ref, a_ref, ga_ref, m_ref, wo_ref, y_ref):
    a = (a_ref[0].astype(F32) * ga_ref[0].astype(F32)).astype(BF16)
    y_ref[0] = (x_ref[0] + _dot(a, wo_ref[0:D_ATT, :]) + _dot(m_ref[0], wo_ref[D_ATT:D_ATT + D_CHK, :]))


def _merge(x, a, ga, m, wo, tm):
    B, S, _ = x.shape
    row_blk = lambda width: pl.BlockSpec((1, tm, width), lambda b, s: (b, s, 0))
    return pl.pallas_call(
        _merge_kernel,
        out_shape=jax.ShapeDtypeStruct((B, S, D_MODEL), F32),
        grid=(B, S // tm),
        in_specs=[row_blk(D_MODEL), row_blk(D_ATT), row_blk(D_ATT), row_blk(D_CHK),
                  pl.BlockSpec((D_ATT + D_CHK, D_MODEL), lambda b, s: (0, 0))],
        out_specs=row_blk(D_MODEL),
        compiler_params=pltpu.CompilerParams(
            dimension_semantics=("arbitrary", "arbitrary"), vmem_limit_bytes=VMEM_LIMIT_BYTES),
        name="merge",
    )(x, a, ga, m, wo)


def _sample_proj_kernel(x_ref, gn_ref, w_ref, wq_ref, bf_ref, gq_ref, gk_ref, gv_ref, w00_ref, b0_ref,
                        k_out, v_out, lf_out, gvn_out, q_out, ga_out, m_out):
    h = _rms_rows(x_ref[...], gn_ref[...]).astype(BF16)
    q_out[...] = _rms_head_pairs(_dot(h, wq_ref[...]), gq_ref[...]) * SCALE
    k_out[...] = _rms_head_pairs(_dot(h, w_ref[:, W_K:W_K + D_ATT]), gk_ref[...])
    v_out[...] = _dot(h, w_ref[:, W_V:W_V + D_ATT])
    lf = _log_sigmoid(_dot(h, w_ref[:, W_F:W_F + LANES]) + bf_ref[...])
    lf_out[...] = lf[:, :N_HEADS]
    ga_out[...] = _silu(_dot(h, w_ref[:, W_ZA:W_ZA + D_ATT])).astype(BF16)
    u = _dot(h, w_ref[:, W_U:W_U + D_CHK])
    gvn = _rms_groups(_dot(h, w_ref[:, W_GV:W_GV + D_CHK]), gv_ref[...])
    gvn_out[...] = gvn
    gate_c = _silu(_dot(h, w_ref[:, W_ZC:W_ZC + D_CHK]))
    m_out[...] = (u * (w00_ref[...] * gvn + b0_ref[...]) * gate_c).astype(BF16)


def _sample_proj(x, gn, w, wq, bf128, gq, gk, gv, w00, b0):
    n = x.shape[0]
    full = lambda a: pl.BlockSpec(a.shape, lambda i: (0,) * a.ndim)
    args = (x, gn, w, wq, bf128, gq, gk, gv, w00, b0)
    out_shape = (
        jax.ShapeDtypeStruct((n, D_ATT), F32), jax.ShapeDtypeStruct((n, D_ATT), F32),
        jax.ShapeDtypeStruct((n, N_HEADS), F32), jax.ShapeDtypeStruct((n, D_CHK), F32),
        jax.ShapeDtypeStruct((n, D_ATT), F32), jax.ShapeDtypeStruct((n, D_ATT), BF16),
        jax.ShapeDtypeStruct((n, D_CHK), BF16),
    )
    return pl.pallas_call(
        _sample_proj_kernel,
        out_shape=out_shape,
        grid=(1,),
        in_specs=[full(a) for a in args],
        out_specs=tuple(pl.BlockSpec(o.shape, lambda i: (0, 0)) for o in out_shape),
        compiler_params=pltpu.CompilerParams(
            dimension_semantics=("arbitrary",), vmem_limit_bytes=VMEM_LIMIT_BYTES),
        name="sample_proj",
    )(*args)


def _page_suffix_matrix():
    idx = np.arange(N_HEADS * PAGE)
    src_h, src_pos = idx // PAGE, idx % PAGE
    dst_h, dst_pos = idx // PAGE, idx % PAGE
    same = src_h[:, None] == dst_h[None, :]
    within = same & (src_pos[:, None] > dst_pos[None, :])
    return np.concatenate([within, same], axis=1).astype(np.float32)


def _page_suffix_kernel(lf_ref, mat_ref, o_ref):
    a1, a2, a3 = _split3(lf_ref[...])
    mat = mat_ref[...]
    o_ref[...] = _dot(a1, mat) + _dot(a2, mat) + _dot(a3, mat)


def _page_suffix(lf2d, mat):
    n = lf2d.shape[0]
    width = PAGE * N_HEADS
    return pl.pallas_call(
        _page_suffix_kernel,
        out_shape=jax.ShapeDtypeStruct((n, 2 * width), F32),
        grid=(n // PP_ROWS,),
        in_specs=[pl.BlockSpec((PP_ROWS, width), lambda i: (i, 0)),
                  pl.BlockSpec((width, 2 * width), lambda i: (0, 0))],
        out_specs=pl.BlockSpec((PP_ROWS, 2 * width), lambda i: (i, 0)),
        compiler_params=pltpu.CompilerParams(
            dimension_semantics=("arbitrary",), vmem_limit_bytes=VMEM_LIMIT_BYTES),
        name="page_suffix",
    )(lf2d, mat)


def _sample_attn_kernel(n_pages, pt_ref, q_ref, kn_ref, vn_ref, lfn_ref, *refs):
    del pt_ref
    k_refs = refs[0:n_pages]
    v_refs = refs[n_pages:2 * n_pages]
    pp_refs = refs[2 * n_pages:3 * n_pages]
    o_ref = refs[3 * n_pages]

    row = lax.broadcasted_iota(jnp.int32, (N_HEADS, D_ATT), 0)
    col_head = lax.broadcasted_iota(jnp.int32, (N_HEADS, D_ATT), 1) // HEAD_DIM
    own = row == col_head
    qbd = jnp.where(own, jnp.broadcast_to(q_ref[0], (N_HEADS, D_ATT)), 0.0)
    qbd16 = qbd.astype(BF16)

    off = jnp.broadcast_to(lfn_ref[0], (N_HEADS, PAGE))
    scores = [None] * n_pages
    for p in range(n_pages - 1, -1, -1):
        kp = k_refs[p][0].astype(BF16)
        scores[p] = _dot(qbd16, kp) + pp_refs[p][0, 0] + off
        off = off + pp_refs[p][0, 1]
    s_self = jnp.sum(qbd * kn_ref[0], axis=-1, keepdims=True)

    m = s_self
    for p in range(n_pages):
        m = jnp.maximum(m, jnp.max(scores[p], axis=-1, keepdims=True))
    p_self = jnp.exp(s_self - m)
    l = p_self
    acc = p_self * vn_ref[0]
    for p in range(n_pages):
        pr = jnp.exp(scores[p] - m)
        l = l + jnp.sum(pr, axis=-1, keepdims=True)
        acc = acc + _dot_nt(pr.astype(BF16), v_refs[p][0].astype(BF16))
    out = jnp.where(own, acc / l, 0.0)
    o_ref[0] = jnp.sum(out, axis=0, keepdims=True)


def _sample_attn(page_table, q, kn, vn, lfn, cache_k, cache_v, pp):
    nb, n_pages = page_table.shape
    kv_spec = lambda j: pl.BlockSpec((1, D_ATT, PAGE), lambda b, pt: (pt[b, j], 0, 0))
    pp_spec = lambda j: pl.BlockSpec((1, 2, N_HEADS, PAGE), lambda b, pt: (pt[b, j], 0, 0, 0))
    tok = lambda width: pl.BlockSpec((1, 1, width), lambda b, pt: (b, 0, 0))
    in_specs = ([tok(D_ATT), tok(D_ATT), tok(D_ATT),
                 pl.BlockSpec((1, N_HEADS, 1), lambda b, pt: (b, 0, 0))]
                + [kv_spec(j) for j in range(n_pages)]
                + [kv_spec(j) for j in range(n_pages)]
                + [pp_spec(j) for j in range(n_pages)])
    grid_spec = pltpu.PrefetchScalarGridSpec(
        num_scalar_prefetch=1, grid=(nb,), in_specs=in_specs, out_specs=tok(D_ATT))
    return pl.pallas_call(
        functools.partial(_sample_attn_kernel, n_pages),
        out_shape=jax.ShapeDtypeStruct((nb, 1, D_ATT), F32),
        grid_spec=grid_spec,
        compiler_params=pltpu.CompilerParams(
            dimension_semantics=("arbitrary",), vmem_limit_bytes=VMEM_LIMIT_BYTES),
        name="sample_attn",
    )(page_table, q, kn, vn, lfn, *([cache_k] * n_pages), *([cache_v] * n_pages), *([pp] * n_pages))


def _layer(xp, xs, cache_k, cache_v, cache_logf, page_table, g_norm, w_in, b_f, g_q, g_k, g_v,
           w_s, b_s, w_out):
    B, S, _ = xp.shape
    nb = xs.shape[0]
    n_phys = cache_k.shape[0]

    cols = lambda a, n: w_in[:, a:a + n]
    wf = cols(_F0, N_HEADS)
    wf_pad = jnp.concatenate([wf, wf, wf, jnp.zeros((D_MODEL, LANES - 3 * N_HEADS), F32)], axis=1)
    w = jnp.concatenate([cols(_K0, D_ATT), cols(_V0, D_ATT), cols(_ZA0, D_ATT), cols(_U0, D_CHK),
                         cols(_GV0, D_CHK), cols(_ZC0, D_CHK), wf_pad], axis=1).astype(BF16)
    wq = cols(_Q0, D_ATT).astype(BF16)
    wt = jnp.concatenate([cols(_Q0, D_ATT), cols(_V0, D_ATT), wf, jnp.zeros((D_MODEL, 8), F32)],
                         axis=1).T.astype(BF16)
    wo = w_out.astype(BF16)
    gn = g_norm.reshape(1, D_MODEL)
    bf128 = jnp.concatenate([b_f, b_f, b_f, jnp.zeros((LANES - 3 * N_HEADS,), F32)]).reshape(1, LANES)
    bft = b_f.reshape(N_HEADS, 1)
    gq = jnp.tile(g_q, N_HEADS).reshape(1, D_ATT)
    gqt = gq.reshape(D_ATT, 1)
    gk = jnp.tile(g_k, N_HEADS).reshape(1, D_ATT)
    gv = g_v.reshape(1, D_CHK)
    bsb = jnp.broadcast_to(b_s[:, :, None], (N_GROUPS, CHUNK, CH_GROUP))
    w00 = jnp.repeat(w_s[:, 0, 0], CH_GROUP).reshape(1, D_CHK)
    b0 = jnp.repeat(b_s[:, 0], CH_GROUP).reshape(1, D_CHK)
    tri = np.tril(np.ones((TM, TM), np.float32))
    ltri = jnp.asarray(tri, BF16)
    utri = jnp.asarray(tri.T, BF16)

    k_p, v_p, lf_p, kb, e, qt, ext, va, ga, m = _prompt_proj(
        xp, gn, w, wt, bf128, bft, gqt, gk, gv, w_s, bsb, ltri, utri)
    attn = _prompt_attn(qt, ext, kb, e, va)
    y_p = _merge(xp, attn, ga, m, wo, TM)

    xs2 = xs.reshape(nb, D_MODEL)
    k_s, v_s, lf_s, gvn_s, q_s, ga_s, m_s = _sample_proj(xs2, gn, w, wq, bf128, gq, gk, gv, w00, b0)
    lf_pages = jnp.transpose(cache_logf, (0, 2, 1)).reshape(n_phys, N_HEADS * PAGE)
    kt_pages = jnp.transpose(cache_k, (0, 2, 3, 1)).reshape(n_phys, D_ATT, PAGE)
    vt_pages = jnp.transpose(cache_v, (0, 2, 3, 1)).reshape(n_phys, D_ATT, PAGE)
    pp = _page_suffix(lf_pages, jnp.asarray(_page_suffix_matrix(), BF16))
    pp = pp.reshape(n_phys, 2, N_HEADS, PAGE)
    attn_s = _sample_attn(page_table, q_s.reshape(nb, 1, D_ATT), k_s.reshape(nb, 1, D_ATT),
                          v_s.reshape(nb, 1, D_ATT), lf_s.reshape(nb, N_HEADS, 1),
                          kt_pages, vt_pages, pp)
    y_s = _merge(xs2.reshape(1, nb, D_MODEL), attn_s.reshape(1, nb, D_ATT).astype(BF16),
                 ga_s.reshape(1, nb, D_ATT), m_s.reshape(1, nb, D_CHK), wo, nb)

    return (y_p, y_s.reshape(nb, 1, D_MODEL),
            k_p.reshape(B, S, N_HEADS, HEAD_DIM), v_p.reshape(B, S, N_HEADS, HEAD_DIM), lf_p,
            k_s.reshape(nb, 1, N_HEADS, HEAD_DIM), v_s.reshape(nb, 1, N_HEADS, HEAD_DIM),
            lf_s.reshape(nb, 1, N_HEADS), gvn_s.reshape(nb, 1, N_GROUPS, CH_GROUP))


def kernel(x_prompt, x_sample, cache_k, cache_v, cache_logf, page_table, g_norm, w_in, b_f, g_q, g_k,
           g_v, w_s, b_s, w_out):
    depth = w_in.shape[0]
    assert x_sample.shape[1] == 1, "one new token per decode sequence"
    xp, xs = x_prompt, x_sample
    outs = []
    for l in range(depth):
        res = _layer(xp, xs, cache_k[l], cache_v[l], cache_logf[l], page_table, g_norm[l], w_in[l],
                     b_f[l], g_q[l], g_k[l], g_v[l], w_s[l], b_s[l], w_out[l])
        xp, xs = res[0], res[1]
        outs.append(res[2:])
    stacked = tuple(jnp.stack([o[i] for o in outs]) for i in range(7))
    return (xp, xs) + stacked
```

```python
import functools

import numpy as np
import jax
import jax.numpy as jnp
from jax import lax
from jax.experimental import pallas as pl
from jax.experimental.pallas import tpu as pltpu

F32 = jnp.float32
BF16 = jnp.bfloat16

D_MODEL = 1024
N_HEADS = 8
HEAD_DIM = 64
D_ATT = N_HEADS * HEAD_DIM
N_GROUPS = 4
CH_GROUP = 128
D_CHK = N_GROUPS * CH_GROUP
CHUNK = 128
PAGE = 128
EPS = 1e-6
SCALE = HEAD_DIM ** -0.5
LOG2E = 1.4426950408889634

LANES = 128
VMEM_LIMIT_BYTES = 56 * 1024 * 1024

TM = 512
TQ = 512
TK = 256
KS = TQ
V_ROWS = 80
X_ROWS = 32
PP_ROWS = 256

_Q0, _K0, _V0, _F0 = 0, D_ATT, 2 * D_ATT, 3 * D_ATT
_ZA0 = 3 * D_ATT + N_HEADS
_U0 = _ZA0 + D_ATT
_GV0 = _U0 + D_CHK
_ZC0 = _GV0 + D_CHK
W_K, W_V, W_ZA, W_U, W_GV, W_ZC, W_F = (i * 512 for i in range(7))
W_COLS = W_F + LANES
WT_ROWS = 2 * D_ATT + 16


def _log_sigmoid(x):
    return jnp.minimum(x, 0.0) - jnp.log1p(jnp.exp(-jnp.abs(x)))


def _silu(x):
    return x / (1.0 + jnp.exp(-x))


def _rms_rows(x, gain):
    ms = jnp.sum(x * x, axis=-1, keepdims=True) * (1.0 / x.shape[-1])
    return x * lax.rsqrt(ms + EPS) * gain


def _rms_head_pairs(x, gain):
    outs = []
    for p in range(x.shape[-1] // LANES):
        xp = x[:, p * LANES:(p + 1) * LANES]
        sq = xp * xp
        lo = lax.broadcasted_iota(jnp.int32, xp.shape, 1) < HEAD_DIM
        s_lo = jnp.sum(jnp.where(lo, sq, 0.0), axis=-1, keepdims=True)
        s_hi = jnp.sum(jnp.where(lo, 0.0, sq), axis=-1, keepdims=True)
        r = jnp.where(lo, lax.rsqrt(s_lo * (1.0 / HEAD_DIM) + EPS),
                      lax.rsqrt(s_hi * (1.0 / HEAD_DIM) + EPS))
        outs.append(xp * r * gain[:, p * LANES:(p + 1) * LANES])
    return jnp.concatenate(outs, axis=-1)


def _rms_groups(x, gain):
    outs = []
    for g in range(N_GROUPS):
        xg = x[:, g * CH_GROUP:(g + 1) * CH_GROUP]
        outs.append(_rms_rows(xg, gain[:, g * CH_GROUP:(g + 1) * CH_GROUP]))
    return jnp.concatenate(outs, axis=-1)


def _split3(a):
    a1 = a.astype(BF16)
    r1 = a - a1.astype(F32)
    a2 = r1.astype(BF16)
    a3 = (r1 - a2.astype(F32)).astype(BF16)
    return a1, a2, a3


def _dot(a, b):
    return jnp.dot(a, b, preferred_element_type=F32)


def _dot_nt(a, b):
    return lax.dot_general(a, b, (((1,), (1,)), ((), ())), preferred_element_type=F32)


def _prompt_proj_kernel(x_ref, gn_ref, w_ref, wt_ref, bf_ref, bft_ref, gqt_ref, gk_ref, gv_ref,
                        ws_ref, bsb_ref, ltri_ref, utri_ref,
                        k_out, v_out, lf_out, kb_out, e_out, qt_out, ext_out, va_out, ga_out, m_out,
                        carry_n, carry_t):
    s_idx = pl.program_id(1)

    @pl.when(s_idx == 0)
    def _():
        carry_n[...] = jnp.zeros_like(carry_n)
        carry_t[...] = jnp.zeros_like(carry_t)

    x = x_ref[0]
    h = _rms_rows(x, gn_ref[...]).astype(BF16)

    k = _rms_head_pairs(_dot(h, w_ref[:, W_K:W_K + D_ATT]), gk_ref[...])
    k_out[0] = k
    kb_out[0] = k.astype(BF16)
    v_out[0] = _dot(h, w_ref[:, W_V:W_V + D_ATT])

    lf = _log_sigmoid(_dot(h, w_ref[:, W_F:W_F + LANES]) + bf_ref[...])
    lf_out[0] = lf[:, :N_HEADS]
    l1, l2, l3 = _split3(lf)
    ltri = ltri_ref[...]
    c = carry_n[...] + (_dot(ltri, l1) + _dot(ltri, l2) + _dot(ltri, l3))
    carry_n[...] = c[TM - 1:TM, :]
    c1, c2, c3 = _split3(c * LOG2E)
    lane = lax.broadcasted_iota(jnp.int32, c.shape, 1)
    e = jnp.where(lane < 8, -c1.astype(F32),
                  jnp.where(lane < 16, -c2.astype(F32),
                            jnp.where(lane < 24, -c3.astype(F32),
                                      jnp.where(lane < 27, 1.0, 0.0))))
    e_out[0] = e.astype(BF16)

    t = _dot_nt(wt_ref[...], h)
    qt_out[0] = (_rms_heads_t(t[0:D_ATT], gqt_ref[...]) * (SCALE * LOG2E)).astype(BF16)

    vt = t[D_ATT:2 * D_ATT]
    row16 = lax.broadcasted_iota(jnp.int32, (V_ROWS - HEAD_DIM, TK), 0)
    tail = jnp.where(row16 == 0, 1.0, 0.0)
    for hh in range(N_HEADS):
        for kb in range(TM // TK):
            blk = vt[hh * HEAD_DIM:(hh + 1) * HEAD_DIM, kb * TK:(kb + 1) * TK]
            va_out[0, hh, kb] = jnp.concatenate([blk, tail], axis=0).astype(BF16)

    lft = _log_sigmoid(t[2 * D_ATT:2 * D_ATT + N_HEADS] + bft_ref[...])
    t1, t2, t3 = _split3(lft)
    utri = utri_ref[...]
    ct = carry_t[:, 0:1] + (_dot(t1, utri) + _dot(t2, utri) + _dot(t3, utri))
    carry_t[...] = jnp.broadcast_to(ct[:, TM - 1:TM], carry_t.shape)
    ct1, ct2, ct3 = _split3(ct * LOG2E)
    rid = lax.broadcasted_iota(jnp.int32, (X_ROWS, TM), 0)
    for hh in range(N_HEADS):
        sel = jnp.where(rid < 24, jnp.where((rid & 7) == hh, 1.0, 0.0), 0.0)
        ext = jnp.where(rid == 24, ct1[hh:hh + 1].astype(F32),
                        jnp.where(rid == 25, ct2[hh:hh + 1].astype(F32),
                                  jnp.where(rid == 26, ct3[hh:hh + 1].astype(F32), sel)))
        ext_out[0, hh] = ext.astype(BF16)

    ga_out[0] = _silu(_dot(h, w_ref[:, W_ZA:W_ZA + D_ATT])).astype(BF16)
    u = _dot(h, w_ref[:, W_U:W_U + D_CHK])
    gvn = _rms_groups(_dot(h, w_ref[:, W_GV:W_GV + D_CHK]), gv_ref[...]).astype(BF16)
    gate_c = _silu(_dot(h, w_ref[:, W_ZC:W_ZC + D_CHK]))
    tri = (lax.broadcasted_iota(jnp.int32, (CHUNK, CHUNK), 1)
           <= lax.broadcasted_iota(jnp.int32, (CHUNK, CHUNK), 0))
    for g in range(N_GROUPS):
        wg = jnp.where(tri, ws_ref[g], 0.0).astype(BF16)
        cs = slice(g * CH_GROUP, (g + 1) * CH_GROUP)
        for cc in range(TM // CHUNK):
            rs = slice(cc * CHUNK, (cc + 1) * CHUNK)
            mix = _dot(wg, gvn[rs, cs]) + bsb_ref[g]
            m_out[0, rs, cs] = (u[rs, cs] * mix * gate_c[rs, cs]).astype(BF16)


def _prompt_proj(x, gn, w, wt, bf128, bft, gqt, gk, gv, ws, bsb, ltri, utri):
    B, S, _ = x.shape
    nk = S // TK
    const2 = lambda shape: pl.BlockSpec(shape, lambda b, s: (0, 0))
    const3 = lambda shape: pl.BlockSpec(shape, lambda b, s: (0, 0, 0))
    row_blk = lambda width: pl.BlockSpec((1, TM, width), lambda b, s: (b, s, 0))
    out_shape = (
        jax.ShapeDtypeStruct((B, S, D_ATT), F32),
        jax.ShapeDtypeStruct((B, S, D_ATT), F32),
        jax.ShapeDtypeStruct((B, S, N_HEADS), F32),
        jax.ShapeDtypeStruct((B, S, D_ATT), BF16),
        jax.ShapeDtypeStruct((B, S, LANES), BF16),
        jax.ShapeDtypeStruct((B, D_ATT, S), BF16),
        jax.ShapeDtypeStruct((B, N_HEADS, X_ROWS, S), BF16),
        jax.ShapeDtypeStruct((B, N_HEADS, nk, V_ROWS, TK), BF16),
        jax.ShapeDtypeStruct((B, S, D_ATT), BF16),
        jax.ShapeDtypeStruct((B, S, D_CHK), BF16),
    )
    out_specs = (
        row_blk(D_ATT), row_blk(D_ATT), row_blk(N_HEADS), row_blk(D_ATT), row_blk(LANES),
        pl.BlockSpec((1, D_ATT, TM), lambda b, s: (b, 0, s)),
        pl.BlockSpec((1, N_HEADS, X_ROWS, TM), lambda b, s: (b, 0, 0, s)),
        pl.BlockSpec((1, N_HEADS, TM // TK, V_ROWS, TK), lambda b, s: (b, 0, s, 0, 0)),
        row_blk(D_ATT), row_blk(D_CHK),
    )
    in_specs = [
        row_blk(D_MODEL),
        const2((1, D_MODEL)),
        const2((D_MODEL, W_COLS)),
        const2((WT_ROWS, D_MODEL)),
        const2((1, LANES)),
        const2((N_HEADS, 1)),
        const2((D_ATT, 1)),
        const2((1, D_ATT)),
        const2((1, D_CHK)),
        const3((N_GROUPS, CHUNK, CHUNK)),
        const3((N_GROUPS, CHUNK, CH_GROUP)),
        const2((TM, TM)),
        const2((TM, TM)),
    ]
    return pl.pallas_call(
        _prompt_proj_kernel,
        out_shape=out_shape,
        grid=(B, S // TM),
        in_specs=in_specs,
        out_specs=out_specs,
        scratch_shapes=[pltpu.VMEM((1, LANES), F32), pltpu.VMEM((N_HEADS, LANES), F32)],
        compiler_params=pltpu.CompilerParams(
            dimension_semantics=("arbitrary", "arbitrary"), vmem_limit_bytes=VMEM_LIMIT_BYTES),
        name="prompt_proj",
    )(x, gn, w, wt, bf128, bft, gqt, gk, gv, ws, bsb, ltri, utri)


def _head_sum(x):
    return jnp.sum(x.reshape(N_HEADS, HEAD_DIM, x.shape[-1]), axis=1)


def _per_dim(x):
    return jnp.broadcast_to(x[:, None, :], (N_HEADS, HEAD_DIM, x.shape[-1])).reshape(D_ATT, x.shape[-1])


def _page_copies(pt_ref, seq, slot, kt_hbm, vt_hbm, pp_hbm, kbuf, vbuf, ppbuf, sems):
    copies = []
    for j in range(kbuf.shape[1]):
        page = pt_ref[seq, j]
        copies.append(pltpu.make_async_copy(kt_hbm.at[page], kbuf.at[slot, j], sems.at[slot, 0]))
        copies.append(pltpu.make_async_copy(vt_hbm.at[page], vbuf.at[slot, j], sems.at[slot, 1]))
        copies.append(pltpu.make_async_copy(pp_hbm.at[page], ppbuf.at[slot, j], sems.at[slot, 2]))
    return copies


def _decode_past(seq, slot, q_ref, lfs_ref, kbuf, vbuf, ppbuf, os_ref, st_ref):
    n_pages = kbuf.shape[1]
    n_seq = os_ref.shape[-1]
    lane_is_seq = lambda rows: lax.broadcasted_iota(jnp.int32, (rows, n_seq), 1) == seq
    own = (lax.broadcasted_iota(jnp.int32, (N_HEADS, D_ATT), 0)
           == lax.broadcasted_iota(jnp.int32, (N_HEADS, D_ATT), 1) // HEAD_DIM)
    qbd = jnp.where(own, jnp.broadcast_to(q_ref[pl.ds(seq, 1), :], (N_HEADS, D_ATT)), 0.0)
    lf_new = jnp.sum(jnp.where(lane_is_seq(N_HEADS), lfs_ref[...], 0.0), axis=1, keepdims=True)

    off = jnp.broadcast_to(lf_new, (N_HEADS, PAGE))
    scores = [None] * n_pages
    for p in range(n_pages - 1, -1, -1):
        scores[p] = _dot(qbd, kbuf[slot, p]) + ppbuf[slot, p, 0] + off
        off = off + ppbuf[slot, p, 1]
    top = scores[0]
    for p in range(1, n_pages):
        top = jnp.maximum(top, scores[p])
    m = jnp.max(top, axis=-1, keepdims=True)
    p_sum = jnp.zeros((N_HEADS, PAGE), F32)
    acc = jnp.zeros((D_ATT, PAGE), F32)
    for p in range(n_pages):
        pr = jnp.exp(scores[p] - m)
        p_sum = p_sum + pr
        acc = acc + vbuf[slot, p] * _per_dim(pr)
    os_ref[...] = jnp.where(lane_is_seq(D_ATT), jnp.sum(acc, axis=-1, keepdims=True), os_ref[...])
    stats = jnp.concatenate([m, jnp.sum(p_sum, axis=-1, keepdims=True)], axis=0)
    st_ref[...] = jnp.where(lane_is_seq(2 * N_HEADS), stats, st_ref[...])


def _decode_finish(qts_ref, kts_ref, vts_ref, os_ref, st_ref):
    s_self = _head_sum(qts_ref[...] * kts_ref[...])
    m_past, l_past = st_ref[0:N_HEADS, :], st_ref[N_HEADS:2 * N_HEADS, :]
    m = jnp.maximum(m_past, s_self)
    w_past, p_self = jnp.exp(m_past - m), jnp.exp(s_self - m)
    os_ref[...] = ((os_ref[...] * _per_dim(w_past) + _per_dim(p_self) * vts_ref[...])
                   / _per_dim(l_past * w_past + p_self))


def _prompt_attn_kernel(n_steps, pt_ref, qt_ref, ext_ref, kb_ref, e_ref, va_ref,
                        qs_ref, qts_ref, kts_ref, vts_ref, lfs_ref, kt_hbm, vt_hbm, pp_hbm,
                        o_ref, os_ref, wq_ref, s_ref, p_ref, kbuf, vbuf, ppbuf, st_ref, sems):
    i = pl.program_id(2)
    n_tiles = i + 1

    step = (pl.program_id(0) * pl.num_programs(1) + pl.program_id(1)) * pl.num_programs(2) + i
    slot = lax.rem(step, 2)
    copies = functools.partial(_page_copies, pt_ref, kt_hbm=kt_hbm, vt_hbm=vt_hbm, pp_hbm=pp_hbm,
                               kbuf=kbuf, vbuf=vbuf, ppbuf=ppbuf, sems=sems)

    @pl.when(step == 0)
    def _():
        os_ref[...] = jnp.zeros_like(os_ref)
        st_ref[...] = jnp.zeros_like(st_ref)
        for c in copies(0, 0):
            c.start()

    @pl.when(step + 1 < n_steps)
    def _():
        for c in copies(step + 1, 1 - slot):
            c.start()

    for c in copies(step, slot):
        c.wait()
    _decode_past(step, slot, qs_ref, lfs_ref, kbuf, vbuf, ppbuf, os_ref, st_ref)

    wq_ref[...] = jnp.zeros_like(wq_ref)
    wq_ref[0:HEAD_DIM, 0:TQ] = qt_ref[0, 0:HEAD_DIM, :]
    wq_ref[HEAD_DIM:2 * HEAD_DIM, TQ:2 * TQ] = qt_ref[0, HEAD_DIM:2 * HEAD_DIM, :]
    wq_ref[LANES:LANES + X_ROWS, 0:TQ] = ext_ref[0, 0]
    wq_ref[LANES:LANES + X_ROWS, TQ:2 * TQ] = ext_ref[0, 1]

    visit = lambda k: jnp.where(k == 0, i, k - 1)

    def scores(tile):
        off = pl.multiple_of(tile * KS, KS)
        ke = jnp.concatenate([kb_ref[0, pl.ds(off, KS), :], e_ref[0, pl.ds(off, KS), :]], axis=1)
        s = _dot(ke, wq_ref[...])
        s_ref[...] = s
        return jnp.max(s, axis=0, keepdims=True)

    def probs(m, s, blk_max, valid=None):
        m_new = jnp.maximum(m, blk_max)
        alpha = jnp.exp2(m - m_new)
        if valid is not None:
            alpha = jnp.where(valid, alpha, 1.0)
            p_ref[...] = jnp.exp2(s - jnp.where(valid, m_new, jnp.inf)).astype(BF16)
            return jnp.where(valid, m_new, m), alpha
        p_ref[...] = jnp.exp2(s - m_new).astype(BF16)
        return m_new, alpha

    def accumulate(tile, alpha, accs):
        out = []
        for hh in range(2):
            cols = slice(hh * TQ, (hh + 1) * TQ)
            acc = alpha[:, cols] * accs[hh]
            for kk in range(KS // TK):
                acc = acc + _dot(va_ref[0, hh, tile * (KS // TK) + kk], p_ref[kk * TK:(kk + 1) * TK, cols])
            out.append(acc)
        return tuple(out)

    scores(i)
    krow = lax.broadcasted_iota(jnp.int32, (KS, 2 * TQ), 0)
    qcol = lax.broadcasted_iota(jnp.int32, (KS, 2 * TQ), 1) & (TQ - 1)
    s_diag = jnp.where(krow <= qcol, s_ref[...], -jnp.inf)
    zero = jnp.zeros((V_ROWS, TQ), F32)
    m, alpha = probs(jnp.full((1, 2 * TQ), -jnp.inf, F32), s_diag, jnp.max(s_diag, axis=0, keepdims=True))
    blk_max = scores(0)

    def body(k, carry):
        m, alpha, accs, blk_max = carry
        accs = accumulate(visit(k), alpha, accs)
        m, alpha = probs(m, s_ref[...], blk_max)
        return m, alpha, accs, scores(k + 1)

    m, alpha, accs, blk_max = lax.fori_loop(0, jnp.maximum(n_tiles - 2, 0), body,
                                            (m, alpha, (zero, zero), blk_max))
    accs = accumulate(visit(jnp.maximum(n_tiles - 2, 0)), alpha, accs)
    m, alpha = probs(m, s_ref[...], blk_max, valid=n_tiles >= 2)
    acc0, acc1 = accumulate(visit(n_tiles - 1), alpha, accs)

    o0 = acc0[0:HEAD_DIM] / acc0[HEAD_DIM:HEAD_DIM + 1]
    o1 = acc1[0:HEAD_DIM] / acc1[HEAD_DIM:HEAD_DIM + 1]
    o_ref[0] = jnp.concatenate([o0, o1], axis=0).T.astype(BF16)

    @pl.when(step == n_steps - 1)
    def _():
        _decode_finish(qts_ref, kts_ref, vts_ref, os_ref, st_ref)


def _attention(page_table, qt, ext, kb, e, va, qs, qts, kts, vts, lfs, kt_pages, vt_pages, pp):
    B, _, S = qt.shape
    nk = S // TK
    n_seq, n_pages = page_table.shape
    grid = (B, N_HEADS // 2, S // TQ)
    assert n_seq == grid[0] * grid[1] * grid[2], "one decode sequence per attention grid step"
    n_steps = grid[0] * grid[1] * grid[2]
    assert n_seq == n_steps, "one decode sequence per attention grid step"
    resident = lambda a: pl.BlockSpec(a.shape, lambda b, p, i, pt: (0, 0))
    in_hbm = pl.BlockSpec(memory_space=pl.ANY)
    in_specs = [
        pl.BlockSpec((1, LANES, TQ), lambda b, p, i, pt: (b, p, i)),
        pl.BlockSpec((1, 2, X_ROWS, TQ), lambda b, p, i, pt: (b, p, 0, i)),
        pl.BlockSpec((1, S, LANES), lambda b, p, i, pt: (b, 0, p)),
        pl.BlockSpec((1, S, LANES), lambda b, p, i, pt: (b, 0, 0)),
        pl.BlockSpec((1, 2, nk, V_ROWS, TK), lambda b, p, i, pt: (b, p, 0, 0, 0)),
        resident(qs), resident(qts), resident(kts), resident(vts), resident(lfs),
        in_hbm, in_hbm, in_hbm]
    grid_spec = pltpu.PrefetchScalarGridSpec(
        num_scalar_prefetch=1, grid=grid, in_specs=in_specs,
        out_specs=(pl.BlockSpec((1, TQ, LANES), lambda b, p, i, pt: (b, i, p)), resident(qts)),
        scratch_shapes=[pltpu.VMEM((2 * LANES, 2 * TQ), BF16), pltpu.VMEM((KS, 2 * TQ), F32),
                        pltpu.VMEM((KS, 2 * TQ), BF16),
                        pltpu.VMEM((2, n_pages, D_ATT, PAGE), F32), pltpu.VMEM((2, n_pages, D_ATT, PAGE), F32),
                        pltpu.VMEM((2, n_pages, 2, N_HEADS, PAGE), F32),
                        pltpu.VMEM((2 * N_HEADS, n_seq), F32), pltpu.SemaphoreType.DMA((2, 3))])
    return pl.pallas_call(
        functools.partial(_prompt_attn_kernel, n_steps),
        out_shape=(jax.ShapeDtypeStruct((B, S, D_ATT), BF16), jax.ShapeDtypeStruct((D_ATT, n_seq), F32)),
        grid_spec=grid_spec,
        compiler_params=pltpu.CompilerParams(
            dimension_semantics=("arbitrary", "arbitrary", "arbitrary"),
            vmem_limit_bytes=VMEM_LIMIT_BYTES),
        name="attention",
    )(page_table, qt, ext, kb, e, va, qs, qts, kts, vts, lfs, kt_pages, vt_pages, pp)


def _merge_kernel(x_ref, a_ref, ga_ref, m_ref, wo_ref, y_ref):
    a = (a_ref[0].astype(F32) * ga_ref[0].astype(F32)).astype(BF16)
    y_ref[0] = (x_ref[0] + _dot(a, wo_ref[0:D_ATT, :]) + _dot(m_ref[0], wo_ref[D_ATT:D_ATT + D_CHK, :]))


def _merge(x, a, ga, m, wo, tm):
    B, S, _ = x.shape
    row_blk = lambda width: pl.BlockSpec((1, tm, width), lambda b, s: (b, s, 0))
    return pl.pallas_call(
        _merge_kernel,
        out_shape=jax.ShapeDtypeStruct((B, S, D_MODEL), F32),
        grid=(B, S // tm),
        in_specs=[row_blk(D_MODEL), row_blk(D_ATT), row_blk(D_ATT), row_blk(D_CHK),
                  pl.BlockSpec((D_ATT + D_CHK, D_MODEL), lambda b, s: (0, 0))],
        out_specs=row_blk(D_MODEL),
        compiler_params=pltpu.CompilerParams(
            dimension_semantics=("arbitrary", "arbitrary"), vmem_limit_bytes=VMEM_LIMIT_BYTES),
        name="merge",
    )(x, a, ga, m, wo)


def _rms_heads_t(xt, gain_col):
    n = xt.shape[-1]
    x3 = xt.reshape(N_HEADS, HEAD_DIM, n)
    ssq = jnp.sum(x3 * x3, axis=1, keepdims=True)
    return (x3 * lax.rsqrt(ssq * (1.0 / HEAD_DIM) + EPS)).reshape(D_ATT, n) * gain_col


def _sample_proj_kernel(x_ref, gn_ref, w_ref, wt_ref, bf_ref, bft_ref, gqt_ref, gkt_ref, gk_ref, gv_ref,
                        w00_ref, b0_ref,
                        k_out, v_out, lf_out, gvn_out, qt_out, kt_out, vt_out, lft_out, ga_out, m_out):
    h = _rms_rows(x_ref[...], gn_ref[...]).astype(BF16)
    t = _dot_nt(wt_ref[...], h)
    qt_out[...] = _rms_heads_t(t[0:D_ATT], gqt_ref[...]) * SCALE
    kt_out[...] = _rms_heads_t(t[D_ATT:2 * D_ATT], gkt_ref[...])
    vt_out[...] = t[2 * D_ATT:3 * D_ATT]
    lft_out[...] = _log_sigmoid(t[3 * D_ATT:3 * D_ATT + N_HEADS] + bft_ref[...])
    k_out[...] = _rms_head_pairs(_dot(h, w_ref[:, W_K:W_K + D_ATT]), gk_ref[...])
    v_out[...] = _dot(h, w_ref[:, W_V:W_V + D_ATT])
    lf = _log_sigmoid(_dot(h, w_ref[:, W_F:W_F + LANES]) + bf_ref[...])
    lf_out[...] = lf[:, :N_HEADS]
    ga_out[...] = _silu(_dot(h, w_ref[:, W_ZA:W_ZA + D_ATT])).astype(BF16)
    u = _dot(h, w_ref[:, W_U:W_U + D_CHK])
    gvn = _rms_groups(_dot(h, w_ref[:, W_GV:W_GV + D_CHK]), gv_ref[...])
    gvn_out[...] = gvn
    gate_c = _silu(_dot(h, w_ref[:, W_ZC:W_ZC + D_CHK]))
    m_out[...] = (u * (w00_ref[...] * gvn + b0_ref[...]) * gate_c).astype(BF16)


def _sample_proj(x, gn, w, wt_s, bf128, bft, gqt, gkt, gk, gv, w00, b0):
    n = x.shape[0]
    full = lambda a: pl.BlockSpec(a.shape, lambda i: (0,) * a.ndim)
    args = (x, gn, w, wt_s, bf128, bft, gqt, gkt, gk, gv, w00, b0)
    out_shape = (
        jax.ShapeDtypeStruct((n, D_ATT), F32), jax.ShapeDtypeStruct((n, D_ATT), F32),
        jax.ShapeDtypeStruct((n, N_HEADS), F32), jax.ShapeDtypeStruct((n, D_CHK), F32),
        jax.ShapeDtypeStruct((D_ATT, n), F32), jax.ShapeDtypeStruct((D_ATT, n), F32),
        jax.ShapeDtypeStruct((D_ATT, n), F32), jax.ShapeDtypeStruct((N_HEADS, n), F32),
        jax.ShapeDtypeStruct((n, D_ATT), BF16), jax.ShapeDtypeStruct((n, D_CHK), BF16),
    )
    return pl.pallas_call(
        _sample_proj_kernel,
        out_shape=out_shape,
        grid=(1,),
        in_specs=[full(a) for a in args],
        out_specs=tuple(pl.BlockSpec(o.shape, lambda i: (0, 0)) for o in out_shape),
        compiler_params=pltpu.CompilerParams(
            dimension_semantics=("arbitrary",), vmem_limit_bytes=VMEM_LIMIT_BYTES),
        name="sample_proj",
    )(*args)


def _page_suffix_matrix():
    idx = np.arange(N_HEADS * PAGE)
    src_h, src_pos = idx // PAGE, idx % PAGE
    dst_h, dst_pos = idx // PAGE, idx % PAGE
    same = src_h[:, None] == dst_h[None, :]
    within = same & (src_pos[:, None] > dst_pos[None, :])
    return np.concatenate([within, same], axis=1).astype(np.float32)


def _page_suffix_kernel(lf_ref, mat_ref, o_ref):
    a1, a2, a3 = _split3(lf_ref[...])
    mat = mat_ref[...]
    o_ref[...] = _dot(a1, mat) + _dot(a2, mat) + _dot(a3, mat)


def _page_suffix(lf2d, mat):
    n = lf2d.shape[0]
    width = PAGE * N_HEADS
    return pl.pallas_call(
        _page_suffix_kernel,
        out_shape=jax.ShapeDtypeStruct((n, 2 * width), F32),
        grid=(n // PP_ROWS,),
        in_specs=[pl.BlockSpec((PP_ROWS, width), lambda i: (i, 0)),
                  pl.BlockSpec((width, 2 * width), lambda i: (0, 0))],
        out_specs=pl.BlockSpec((PP_ROWS, 2 * width), lambda i: (i, 0)),
        compiler_params=pltpu.CompilerParams(
            dimension_semantics=("arbitrary",), vmem_limit_bytes=VMEM_LIMIT_BYTES),
        name="page_suffix",
    )(lf2d, mat)


def _layer(xp, xs, cache_k, cache_v, cache_logf, page_table, g_norm, w_in, b_f, g_q, g_k, g_v,
           w_s, b_s, w_out):
    B, S, _ = xp.shape
    nb = xs.shape[0]
    n_phys = cache_k.shape[0]

    cols = lambda a, n: w_in[:, a:a + n]
    wf = cols(_F0, N_HEADS)
    wf_pad = jnp.concatenate([wf, wf, wf, jnp.zeros((D_MODEL, LANES - 3 * N_HEADS), F32)], axis=1)
    w = jnp.concatenate([cols(_K0, D_ATT), cols(_V0, D_ATT), cols(_ZA0, D_ATT), cols(_U0, D_CHK),
                         cols(_GV0, D_CHK), cols(_ZC0, D_CHK), wf_pad], axis=1).astype(BF16)
    pad8 = jnp.zeros((D_MODEL, 8), F32)
    wt = jnp.concatenate([cols(_Q0, D_ATT), cols(_V0, D_ATT), wf, pad8], axis=1).T.astype(BF16)
    wt_s = jnp.concatenate([cols(_Q0, D_ATT), cols(_K0, D_ATT), cols(_V0, D_ATT), wf, pad8],
                           axis=1).T.astype(BF16)
    wo = w_out.astype(BF16)
    gn = g_norm.reshape(1, D_MODEL)
    bf128 = jnp.concatenate([b_f, b_f, b_f, jnp.zeros((LANES - 3 * N_HEADS,), F32)]).reshape(1, LANES)
    bft = b_f.reshape(N_HEADS, 1)
    gq = jnp.tile(g_q, N_HEADS).reshape(1, D_ATT)
    gqt = gq.reshape(D_ATT, 1)
    gk = jnp.tile(g_k, N_HEADS).reshape(1, D_ATT)
    gkt = gk.reshape(D_ATT, 1)
    gv = g_v.reshape(1, D_CHK)
    bsb = jnp.broadcast_to(b_s[:, :, None], (N_GROUPS, CHUNK, CH_GROUP))
    w00 = jnp.repeat(w_s[:, 0, 0], CH_GROUP).reshape(1, D_CHK)
    b0 = jnp.repeat(b_s[:, 0], CH_GROUP).reshape(1, D_CHK)
    tri = np.tril(np.ones((TM, TM), np.float32))
    ltri = jnp.asarray(tri, BF16)
    utri = jnp.asarray(tri.T, BF16)

    k_p, v_p, lf_p, kb, e, qt, ext, va, ga, m = _prompt_proj(
        xp, gn, w, wt, bf128, bft, gqt, gk, gv, w_s, bsb, ltri, utri)
    xs2 = xs.reshape(nb, D_MODEL)
    k_s, v_s, lf_s, gvn_s, qts, kts, vts, lfs, ga_s, m_s = _sample_proj(
        xs2, gn, w, wt_s, bf128, bft, gqt, gkt, gk, gv, w00, b0)

    lf_pages = jnp.transpose(cache_logf, (0, 2, 1)).reshape(n_phys, N_HEADS * PAGE)
    kt_pages = jnp.transpose(cache_k, (0, 2, 3, 1)).reshape(n_phys, D_ATT, PAGE)
    vt_pages = jnp.transpose(cache_v, (0, 2, 3, 1)).reshape(n_phys, D_ATT, PAGE)
    pp = _page_suffix(lf_pages, jnp.asarray(_page_suffix_matrix(), BF16))
    pp = pp.reshape(n_phys, 2, N_HEADS, PAGE)
    attn, attn_s_t = _attention(page_table, qt, ext, kb, e, va, qts.T, qts, kts, vts, lfs,
                                kt_pages, vt_pages, pp)

    y_p = _merge(xp, attn, ga, m, wo, TM)
    y_s = _merge(xs2.reshape(1, nb, D_MODEL), attn_s_t.T.reshape(1, nb, D_ATT).astype(BF16),
                 ga_s.reshape(1, nb, D_ATT), m_s.reshape(1, nb, D_CHK), wo, nb)

    return (y_p, y_s.reshape(nb, 1, D_MODEL),
            k_p.reshape(B, S, N_HEADS, HEAD_DIM), v_p.reshape(B, S, N_HEADS, HEAD_DIM), lf_p,
            k_s.reshape(nb, 1, N_HEADS, HEAD_DIM), v_s.reshape(nb, 1, N_HEADS, HEAD_DIM),
            lf_s.reshape(nb, 1, N_HEADS), gvn_s.reshape(nb, 1, N_GROUPS, CH_GROUP))


def kernel(x_prompt, x_sample, cache_k, cache_v, cache_logf, page_table, g_norm, w_in, b_f, g_q, g_k,
           g_v, w_s, b_s, w_out):
    depth = w_in.shape[0]
    assert x_sample.shape[1] == 1, "one new token per decode sequence"
    xp, xs = x_prompt, x_sample
    outs = []
    for l in range(depth):
        res = _layer(xp, xs, cache_k[l], cache_v[l], cache_logf[l], page_table, g_norm[l], w_in[l],
                     b_f[l], g_q[l], g_k[l], g_v[l], w_s[l], b_s[l], w_out[l])
        xp, xs = res[0], res[1]
        outs.append(res[2:])
    stacked = tuple(jnp.stack([o[i] for o in outs]) for i in range(7))
    return (xp, xs) + stacked
```

```python
import functools

import numpy as np
import jax
import jax.numpy as jnp
from jax import lax
from jax.experimental import pallas as pl
from jax.experimental.pallas import tpu as pltpu

F32 = jnp.float32
BF16 = jnp.bfloat16

D_MODEL = 1024
N_HEADS = 8
HEAD_DIM = 64
D_ATT = N_HEADS * HEAD_DIM
N_GROUPS = 4
CH_GROUP = 128
D_CHK = N_GROUPS * CH_GROUP
CHUNK = 128
PAGE = 128
EPS = 1e-6
SCALE = HEAD_DIM ** -0.5
LOG2E = 1.4426950408889634

LANES = 128
VMEM_LIMIT_BYTES = 56 * 1024 * 1024

TM = 512
TQ = 512
TK = 256
KS = TQ
V_ROWS = 80
X_ROWS = 32
PP_ROWS = 2048

_Q0, _K0, _V0, _F0 = 0, D_ATT, 2 * D_ATT, 3 * D_ATT
_ZA0 = 3 * D_ATT + N_HEADS
_U0 = _ZA0 + D_ATT
_GV0 = _U0 + D_CHK
_ZC0 = _GV0 + D_CHK
W_K, W_ZA, W_U, W_GV, W_ZC = (i * 512 for i in range(5))
W_COLS = 5 * 512
WT_ROWS = 2 * D_ATT + 16


def _log_sigmoid(x):
    return jnp.minimum(x, 0.0) - jnp.log1p(jnp.exp(-jnp.abs(x)))


def _silu(x):
    return x / (1.0 + jnp.exp(-x))


def _rms_rows(x, gain):
    ms = jnp.sum(x * x, axis=-1, keepdims=True) * (1.0 / x.shape[-1])
    return x * lax.rsqrt(ms + EPS) * gain


def _rms_head_pairs(x, gain):
    outs = []
    for p in range(x.shape[-1] // LANES):
        xp = x[:, p * LANES:(p + 1) * LANES]
        sq = xp * xp
        lo = lax.broadcasted_iota(jnp.int32, xp.shape, 1) < HEAD_DIM
        s_lo = jnp.sum(jnp.where(lo, sq, 0.0), axis=-1, keepdims=True)
        s_hi = jnp.sum(jnp.where(lo, 0.0, sq), axis=-1, keepdims=True)
        r = jnp.where(lo, lax.rsqrt(s_lo * (1.0 / HEAD_DIM) + EPS),
                      lax.rsqrt(s_hi * (1.0 / HEAD_DIM) + EPS))
        outs.append(xp * r * gain[:, p * LANES:(p + 1) * LANES])
    return jnp.concatenate(outs, axis=-1)


def _rms_groups(x, gain):
    outs = []
    for g in range(N_GROUPS):
        xg = x[:, g * CH_GROUP:(g + 1) * CH_GROUP]
        outs.append(_rms_rows(xg, gain[:, g * CH_GROUP:(g + 1) * CH_GROUP]))
    return jnp.concatenate(outs, axis=-1)


def _split3(a):
    a1 = a.astype(BF16)
    r1 = a - a1.astype(F32)
    a2 = r1.astype(BF16)
    a3 = (r1 - a2.astype(F32)).astype(BF16)
    return a1, a2, a3


def _dot(a, b):
    return jnp.dot(a, b, preferred_element_type=F32)


def _dot_nt(a, b):
    return lax.dot_general(a, b, (((1,), (1,)), ((), ())), preferred_element_type=F32)


def _prompt_proj_kernel(x_ref, gn_ref, w_ref, wt_ref, bft_ref, gqt_ref, gk_ref, gv_ref,
                        ws_ref, bsb_ref, utri_ref,
                        k_out, vt_out, lft_out, kb_out, e_out, qt_out, ext_out, va_out, ga_out, m_out,
                        carry_t):
    s_idx = pl.program_id(1)

    @pl.when(s_idx == 0)
    def _():
        carry_t[...] = jnp.zeros_like(carry_t)

    x = x_ref[0]
    h = _rms_rows(x, gn_ref[...]).astype(BF16)

    k = _rms_head_pairs(_dot(h, w_ref[:, W_K:W_K + D_ATT]), gk_ref[...])
    k_out[0] = k
    kb_out[0] = k.astype(BF16)

    t = _dot_nt(wt_ref[...], h)
    qt_out[0] = (_rms_heads_t(t[0:D_ATT], gqt_ref[...]) * (SCALE * LOG2E)).astype(BF16)

    vt = t[D_ATT:2 * D_ATT]
    vt_out[0] = vt
    row16 = lax.broadcasted_iota(jnp.int32, (V_ROWS - HEAD_DIM, TK), 0)
    tail = jnp.where(row16 == 0, 1.0, 0.0)
    for hh in range(N_HEADS):
        for kb in range(TM // TK):
            blk = vt[hh * HEAD_DIM:(hh + 1) * HEAD_DIM, kb * TK:(kb + 1) * TK]
            va_out[0, hh, kb] = jnp.concatenate([blk, tail], axis=0).astype(BF16)

    lft = _log_sigmoid(t[2 * D_ATT:2 * D_ATT + N_HEADS] + bft_ref[...])
    lft_out[0] = lft
    t1, t2, t3 = _split3(lft)
    utri = utri_ref[...]
    ct = carry_t[:, 0:1] + (_dot(t1, utri) + _dot(t2, utri) + _dot(t3, utri))
    carry_t[...] = jnp.broadcast_to(ct[:, TM - 1:TM], carry_t.shape)
    ct1, ct2, ct3 = (piece.astype(F32) for piece in _split3(ct * LOG2E))
    rid = lax.broadcasted_iota(jnp.int32, (X_ROWS, TM), 0)
    for hh in range(N_HEADS):
        sel = jnp.where(rid < 24, jnp.where((rid & 7) == hh, 1.0, 0.0), 0.0)
        ext = jnp.where(rid == 24, ct1[hh:hh + 1],
                        jnp.where(rid == 25, ct2[hh:hh + 1], jnp.where(rid == 26, ct3[hh:hh + 1], sel)))
        ext_out[0, hh] = ext.astype(BF16)
    rid8 = lax.broadcasted_iota(jnp.int32, (N_HEADS, TM), 0)
    ones3 = jnp.where(rid8 < 3, 1.0, 0.0)
    e_t = jnp.concatenate([-ct1, -ct2, -ct3, ones3, jnp.zeros((LANES - 4 * N_HEADS, TM), F32)], axis=0)
    e_out[0] = e_t.T.astype(BF16)

    ga_out[0] = _silu(_dot(h, w_ref[:, W_ZA:W_ZA + D_ATT])).astype(BF16)
    u = _dot(h, w_ref[:, W_U:W_U + D_CHK])
    gvn = _rms_groups(_dot(h, w_ref[:, W_GV:W_GV + D_CHK]), gv_ref[...]).astype(BF16)
    gate_c = _silu(_dot(h, w_ref[:, W_ZC:W_ZC + D_CHK]))
    tri = (lax.broadcasted_iota(jnp.int32, (CHUNK, CHUNK), 1)
           <= lax.broadcasted_iota(jnp.int32, (CHUNK, CHUNK), 0))
    for g in range(N_GROUPS):
        wg = jnp.where(tri, ws_ref[g], 0.0).astype(BF16)
        cs = slice(g * CH_GROUP, (g + 1) * CH_GROUP)
        for cc in range(TM // CHUNK):
            rs = slice(cc * CHUNK, (cc + 1) * CHUNK)
            mix = _dot(wg, gvn[rs, cs]) + bsb_ref[g]
            m_out[0, rs, cs] = (u[rs, cs] * mix * gate_c[rs, cs]).astype(BF16)


def _prompt_proj(x, gn, w, wt, bft, gqt, gk, gv, ws, bsb, utri):
    B, S, _ = x.shape
    nk = S // TK
    const2 = lambda shape: pl.BlockSpec(shape, lambda b, s: (0, 0))
    const3 = lambda shape: pl.BlockSpec(shape, lambda b, s: (0, 0, 0))
    row_blk = lambda width: pl.BlockSpec((1, TM, width), lambda b, s: (b, s, 0))
    col_blk = lambda rows: pl.BlockSpec((1, rows, TM), lambda b, s: (b, 0, s))
    out_shape = (
        jax.ShapeDtypeStruct((B, S, D_ATT), F32),
        jax.ShapeDtypeStruct((B, D_ATT, S), F32),
        jax.ShapeDtypeStruct((B, N_HEADS, S), F32),
        jax.ShapeDtypeStruct((B, S, D_ATT), BF16),
        jax.ShapeDtypeStruct((B, S, LANES), BF16),
        jax.ShapeDtypeStruct((B, D_ATT, S), BF16),
        jax.ShapeDtypeStruct((B, N_HEADS, X_ROWS, S), BF16),
        jax.ShapeDtypeStruct((B, N_HEADS, nk, V_ROWS, TK), BF16),
        jax.ShapeDtypeStruct((B, S, D_ATT), BF16),
        jax.ShapeDtypeStruct((B, S, D_CHK), BF16),
    )
    out_specs = (
        row_blk(D_ATT), col_blk(D_ATT), col_blk(N_HEADS), row_blk(D_ATT), row_blk(LANES),
        col_blk(D_ATT),
        pl.BlockSpec((1, N_HEADS, X_ROWS, TM), lambda b, s: (b, 0, 0, s)),
        pl.BlockSpec((1, N_HEADS, TM // TK, V_ROWS, TK), lambda b, s: (b, 0, s, 0, 0)),
        row_blk(D_ATT), row_blk(D_CHK),
    )
    in_specs = [
        row_blk(D_MODEL),
        const2((1, D_MODEL)),
        const2((D_MODEL, W_COLS)),
        const2((WT_ROWS, D_MODEL)),
        const2((N_HEADS, 1)),
        const2((D_ATT, 1)),
        const2((1, D_ATT)),
        const2((1, D_CHK)),
        const3((N_GROUPS, CHUNK, CHUNK)),
        const3((N_GROUPS, CHUNK, CH_GROUP)),
        const2((TM, TM)),
    ]
    return pl.pallas_call(
        _prompt_proj_kernel,
        out_shape=out_shape,
        grid=(B, S // TM),
        in_specs=in_specs,
        out_specs=out_specs,
        scratch_shapes=[pltpu.VMEM((N_HEADS, LANES), F32)],
        compiler_params=pltpu.CompilerParams(
            dimension_semantics=("arbitrary", "arbitrary"), vmem_limit_bytes=VMEM_LIMIT_BYTES),
        name="prompt_proj",
    )(x, gn, w, wt, bft, gqt, gk, gv, ws, bsb, utri)


def _head_sum(x):
    return jnp.sum(x.reshape(N_HEADS, HEAD_DIM, x.shape[-1]), axis=1)


def _per_dim(x):
    return jnp.broadcast_to(x[:, None, :], (N_HEADS, HEAD_DIM, x.shape[-1])).reshape(D_ATT, x.shape[-1])


def _page_copies(pt_ref, seq, slot, kt_hbm, vt_hbm, pp_hbm, kbuf, vbuf, ppbuf, sems):
    copies = []
    for j in range(kbuf.shape[1]):
        page = pt_ref[seq, j]
        copies.append(pltpu.make_async_copy(kt_hbm.at[page], kbuf.at[slot, j], sems.at[slot, 0]))
        copies.append(pltpu.make_async_copy(vt_hbm.at[page], vbuf.at[slot, j], sems.at[slot, 1]))
        copies.append(pltpu.make_async_copy(pp_hbm.at[page], ppbuf.at[slot, j], sems.at[slot, 2]))
    return copies


def _decode_past(seq, slot, q_ref, lfs_ref, kbuf, vbuf, ppbuf, os_ref, st_ref):
    n_pages = kbuf.shape[1]
    n_seq = os_ref.shape[-1]
    lane_is_seq = lambda rows: lax.broadcasted_iota(jnp.int32, (rows, n_seq), 1) == seq
    own = (lax.broadcasted_iota(jnp.int32, (N_HEADS, D_ATT), 0)
           == lax.broadcasted_iota(jnp.int32, (N_HEADS, D_ATT), 1) // HEAD_DIM)
    qbd = jnp.where(own, jnp.broadcast_to(q_ref[pl.ds(seq, 1), :], (N_HEADS, D_ATT)), 0.0)
    lf_new = jnp.sum(jnp.where(lane_is_seq(N_HEADS), lfs_ref[...], 0.0), axis=1, keepdims=True)

    off = jnp.broadcast_to(lf_new, (N_HEADS, PAGE))
    scores = [None] * n_pages
    for p in range(n_pages - 1, -1, -1):
        scores[p] = _dot(qbd, kbuf[slot, p]) + ppbuf[slot, p, :, 0:PAGE] + off
        off = off + ppbuf[slot, p, :, PAGE:2 * PAGE]
    top = scores[0]
    for p in range(1, n_pages):
        top = jnp.maximum(top, scores[p])
    m = jnp.max(top, axis=-1, keepdims=True)
    p_sum = jnp.zeros((N_HEADS, PAGE), F32)
    acc = jnp.zeros((D_ATT, PAGE), F32)
    for p in range(n_pages):
        pr = jnp.exp(scores[p] - m)
        p_sum = p_sum + pr
        acc = acc + vbuf[slot, p] * _per_dim(pr)
    os_ref[...] = jnp.where(lane_is_seq(D_ATT), jnp.sum(acc, axis=-1, keepdims=True), os_ref[...])
    stats = jnp.concatenate([m, jnp.sum(p_sum, axis=-1, keepdims=True)], axis=0)
    st_ref[...] = jnp.where(lane_is_seq(2 * N_HEADS), stats, st_ref[...])


def _decode_finish(qts_ref, kts_ref, vts_ref, os_ref, st_ref):
    s_self = _head_sum(qts_ref[...] * kts_ref[...])
    m_past, l_past = st_ref[0:N_HEADS, :], st_ref[N_HEADS:2 * N_HEADS, :]
    m = jnp.maximum(m_past, s_self)
    w_past, p_self = jnp.exp(m_past - m), jnp.exp(s_self - m)
    os_ref[...] = ((os_ref[...] * _per_dim(w_past) + _per_dim(p_self) * vts_ref[...])
                   / _per_dim(l_past * w_past + p_self))


def _prompt_attn_kernel(n_steps, pt_ref, qt_ref, ext_ref, kb_ref, e_ref, va_ref,
                        qs_ref, qts_ref, kts_ref, vts_ref, lfs_ref, kt_hbm, vt_hbm, pp_hbm,
                        o_ref, os_ref, wq_ref, s_ref, p_ref, kbuf, vbuf, ppbuf, st_ref, sems):
    i = pl.program_id(2)
    n_tiles = i + 1

    step = (pl.program_id(0) * pl.num_programs(1) + pl.program_id(1)) * pl.num_programs(2) + i
    slot = lax.rem(step, 2)
    copies = functools.partial(_page_copies, pt_ref, kt_hbm=kt_hbm, vt_hbm=vt_hbm, pp_hbm=pp_hbm,
                               kbuf=kbuf, vbuf=vbuf, ppbuf=ppbuf, sems=sems)

    @pl.when(step == 0)
    def _():
        os_ref[...] = jnp.zeros_like(os_ref)
        st_ref[...] = jnp.zeros_like(st_ref)
        for c in copies(0, 0):
            c.start()

    @pl.when(step + 1 < n_steps)
    def _():
        for c in copies(step + 1, 1 - slot):
            c.start()

    for c in copies(step, slot):
        c.wait()
    _decode_past(step, slot, qs_ref, lfs_ref, kbuf, vbuf, ppbuf, os_ref, st_ref)

    wq_ref[...] = jnp.zeros_like(wq_ref)
    wq_ref[0:HEAD_DIM, 0:TQ] = qt_ref[0, 0:HEAD_DIM, :]
    wq_ref[HEAD_DIM:2 * HEAD_DIM, TQ:2 * TQ] = qt_ref[0, HEAD_DIM:2 * HEAD_DIM, :]
    wq_ref[LANES:LANES + X_ROWS, 0:TQ] = ext_ref[0, 0]
    wq_ref[LANES:LANES + X_ROWS, TQ:2 * TQ] = ext_ref[0, 1]

    visit = lambda k: jnp.where(k == 0, i, k - 1)

    def scores(tile):
        off = pl.multiple_of(tile * KS, KS)
        ke = jnp.concatenate([kb_ref[0, pl.ds(off, KS), :], e_ref[0, pl.ds(off, KS), :]], axis=1)
        s = _dot(ke, wq_ref[...])
        s_ref[...] = s
        return jnp.max(s, axis=0, keepdims=True)

    def probs(m, s, blk_max, valid=None):
        m_new = jnp.maximum(m, blk_max)
        alpha = jnp.exp2(m - m_new)
        if valid is not None:
            alpha = jnp.where(valid, alpha, 1.0)
            p_ref[...] = jnp.exp2(s - jnp.where(valid, m_new, jnp.inf)).astype(BF16)
            return jnp.where(valid, m_new, m), alpha
        p_ref[...] = jnp.exp2(s - m_new).astype(BF16)
        return m_new, alpha

    def accumulate(tile, alpha, accs):
        out = []
        for hh in range(2):
            cols = slice(hh * TQ, (hh + 1) * TQ)
            acc = alpha[:, cols] * accs[hh]
            for kk in range(KS // TK):
                acc = acc + _dot(va_ref[0, hh, tile * (KS // TK) + kk], p_ref[kk * TK:(kk + 1) * TK, cols])
            out.append(acc)
        return tuple(out)

    scores(i)
    krow = lax.broadcasted_iota(jnp.int32, (KS, 2 * TQ), 0)
    qcol = lax.broadcasted_iota(jnp.int32, (KS, 2 * TQ), 1) & (TQ - 1)
    s_diag = jnp.where(krow <= qcol, s_ref[...], -jnp.inf)
    zero = jnp.zeros((V_ROWS, TQ), F32)
    m, alpha = probs(jnp.full((1, 2 * TQ), -jnp.inf, F32), s_diag, jnp.max(s_diag, axis=0, keepdims=True))
    blk_max = scores(0)

    def body(k, carry):
        m, alpha, accs, blk_max = carry
        accs = accumulate(visit(k), alpha, accs)
        m, alpha = probs(m, s_ref[...], blk_max)
        return m, alpha, accs, scores(k + 1)

    m, alpha, accs, blk_max = lax.fori_loop(0, jnp.maximum(n_tiles - 2, 0), body,
                                            (m, alpha, (zero, zero), blk_max))
    accs = accumulate(visit(jnp.maximum(n_tiles - 2, 0)), alpha, accs)
    m, alpha = probs(m, s_ref[...], blk_max, valid=n_tiles >= 2)
    acc0, acc1 = accumulate(visit(n_tiles - 1), alpha, accs)

    o0 = acc0[0:HEAD_DIM] / acc0[HEAD_DIM:HEAD_DIM + 1]
    o1 = acc1[0:HEAD_DIM] / acc1[HEAD_DIM:HEAD_DIM + 1]
    o_ref[0] = jnp.concatenate([o0, o1], axis=0).T.astype(BF16)

    @pl.when(step == n_steps - 1)
    def _():
        _decode_finish(qts_ref, kts_ref, vts_ref, os_ref, st_ref)


def _attention(page_table, qt, ext, kb, e, va, qs, qts, kts, vts, lfs, kt_pages, vt_pages, pp):
    B, _, S = qt.shape
    nk = S // TK
    n_seq, n_pages = page_table.shape
    grid = (B, N_HEADS // 2, S // TQ)
    assert n_seq == grid[0] * grid[1] * grid[2], "one decode sequence per attention grid step"
    n_steps = grid[0] * grid[1] * grid[2]
    assert n_seq == n_steps, "one decode sequence per attention grid step"
    resident = lambda a: pl.BlockSpec(a.shape, lambda b, p, i, pt: (0, 0))
    in_hbm = pl.BlockSpec(memory_space=pl.ANY)
    in_specs = [
        pl.BlockSpec((1, LANES, TQ), lambda b, p, i, pt: (b, p, i)),
        pl.BlockSpec((1, 2, X_ROWS, TQ), lambda b, p, i, pt: (b, p, 0, i)),
        pl.BlockSpec((1, S, LANES), lambda b, p, i, pt: (b, 0, p)),
        pl.BlockSpec((1, S, LANES), lambda b, p, i, pt: (b, 0, 0)),
        pl.BlockSpec((1, 2, nk, V_ROWS, TK), lambda b, p, i, pt: (b, p, 0, 0, 0)),
        resident(qs), resident(qts), resident(kts), resident(vts), resident(lfs),
        in_hbm, in_hbm, in_hbm]
    grid_spec = pltpu.PrefetchScalarGridSpec(
        num_scalar_prefetch=1, grid=grid, in_specs=in_specs,
        out_specs=(pl.BlockSpec((1, TQ, LANES), lambda b, p, i, pt: (b, i, p)), resident(qts)),
        scratch_shapes=[pltpu.VMEM((2 * LANES, 2 * TQ), BF16), pltpu.VMEM((KS, 2 * TQ), F32),
                        pltpu.VMEM((KS, 2 * TQ), BF16),
                        pltpu.VMEM((2, n_pages, D_ATT, PAGE), F32), pltpu.VMEM((2, n_pages, D_ATT, PAGE), F32),
                        pltpu.VMEM((2, n_pages, N_HEADS, 2 * PAGE), F32),
                        pltpu.VMEM((2 * N_HEADS, n_seq), F32), pltpu.SemaphoreType.DMA((2, 3))])
    return pl.pallas_call(
        functools.partial(_prompt_attn_kernel, n_steps),
        out_shape=(jax.ShapeDtypeStruct((B, S, D_ATT), BF16), jax.ShapeDtypeStruct((D_ATT, n_seq), F32)),
        grid_spec=grid_spec,
        compiler_params=pltpu.CompilerParams(
            dimension_semantics=("arbitrary", "arbitrary", "arbitrary"),
            vmem_limit_bytes=VMEM_LIMIT_BYTES),
        name="attention",
    )(page_table, qt, ext, kb, e, va, qs, qts, kts, vts, lfs, kt_pages, vt_pages, pp)


def _merge_kernel(x_ref, a_ref, ga_ref, m_ref, wo_ref, y_ref):
    a = (a_ref[0].astype(F32) * ga_ref[0].astype(F32)).astype(BF16)
    y_ref[0] = (x_ref[0] + _dot(a, wo_ref[0:D_ATT, :]) + _dot(m_ref[0], wo_ref[D_ATT:D_ATT + D_CHK, :]))


def _merge(x, a, ga, m, wo, tm):
    B, S, _ = x.shape
    row_blk = lambda width: pl.BlockSpec((1, tm, width), lambda b, s: (b, s, 0))
    return pl.pallas_call(
        _merge_kernel,
        out_shape=jax.ShapeDtypeStruct((B, S, D_MODEL), F32),
        grid=(B, S // tm),
        in_specs=[row_blk(D_MODEL), row_blk(D_ATT), row_blk(D_ATT), row_blk(D_CHK),
                  pl.BlockSpec((D_ATT + D_CHK, D_MODEL), lambda b, s: (0, 0))],
        out_specs=row_blk(D_MODEL),
        compiler_params=pltpu.CompilerParams(
            dimension_semantics=("arbitrary", "arbitrary"), vmem_limit_bytes=VMEM_LIMIT_BYTES),
        name="merge",
    )(x, a, ga, m, wo)


def _rms_heads_t(xt, gain_col):
    n = xt.shape[-1]
    x3 = xt.reshape(N_HEADS, HEAD_DIM, n)
    ssq = jnp.sum(x3 * x3, axis=1, keepdims=True)
    return (x3 * lax.rsqrt(ssq * (1.0 / HEAD_DIM) + EPS)).reshape(D_ATT, n) * gain_col


def _sample_proj_kernel(x_ref, gn_ref, w_ref, wt_ref, bft_ref, gqt_ref, gkt_ref, gk_ref, gv_ref,
                        w00_ref, b0_ref,
                        k_out, gvn_out, qt_out, kt_out, vt_out, lft_out, ga_out, m_out):
    h = _rms_rows(x_ref[...], gn_ref[...]).astype(BF16)
    t = _dot_nt(wt_ref[...], h)
    qt_out[...] = _rms_heads_t(t[0:D_ATT], gqt_ref[...]) * SCALE
    kt_out[...] = _rms_heads_t(t[D_ATT:2 * D_ATT], gkt_ref[...])
    vt_out[...] = t[2 * D_ATT:3 * D_ATT]
    lft_out[...] = _log_sigmoid(t[3 * D_ATT:3 * D_ATT + N_HEADS] + bft_ref[...])
    k_out[...] = _rms_head_pairs(_dot(h, w_ref[:, W_K:W_K + D_ATT]), gk_ref[...])
    ga_out[...] = _silu(_dot(h, w_ref[:, W_ZA:W_ZA + D_ATT])).astype(BF16)
    u = _dot(h, w_ref[:, W_U:W_U + D_CHK])
    gvn = _rms_groups(_dot(h, w_ref[:, W_GV:W_GV + D_CHK]), gv_ref[...])
    gvn_out[...] = gvn
    gate_c = _silu(_dot(h, w_ref[:, W_ZC:W_ZC + D_CHK]))
    m_out[...] = (u * (w00_ref[...] * gvn + b0_ref[...]) * gate_c).astype(BF16)


def _sample_proj(x, gn, w, wt_s, bft, gqt, gkt, gk, gv, w00, b0):
    n = x.shape[0]
    full = lambda a: pl.BlockSpec(a.shape, lambda i: (0,) * a.ndim)
    args = (x, gn, w, wt_s, bft, gqt, gkt, gk, gv, w00, b0)
    out_shape = (
        jax.ShapeDtypeStruct((n, D_ATT), F32), jax.ShapeDtypeStruct((n, D_CHK), F32),
        jax.ShapeDtypeStruct((D_ATT, n), F32), jax.ShapeDtypeStruct((D_ATT, n), F32),
        jax.ShapeDtypeStruct((D_ATT, n), F32), jax.ShapeDtypeStruct((N_HEADS, n), F32),
        jax.ShapeDtypeStruct((n, D_ATT), BF16), jax.ShapeDtypeStruct((n, D_CHK), BF16),
    )
    return pl.pallas_call(
        _sample_proj_kernel,
        out_shape=out_shape,
        grid=(1,),
        in_specs=[full(a) for a in args],
        out_specs=tuple(pl.BlockSpec(o.shape, lambda i: (0, 0)) for o in out_shape),
        compiler_params=pltpu.CompilerParams(
            dimension_semantics=("arbitrary",), vmem_limit_bytes=VMEM_LIMIT_BYTES),
        name="sample_proj",
    )(*args)


def _page_suffix_matrix():
    pos = np.arange(PAGE)
    later = pos[:, None] > pos[None, :]
    return np.concatenate([later, np.ones((PAGE, PAGE), bool)], axis=1).astype(np.float32)


def _page_suffix_kernel(lf_ref, mat_ref, o_ref):
    a1, a2, a3 = _split3(lf_ref[...])
    mat = mat_ref[...]
    o_ref[...] = _dot(a1, mat) + _dot(a2, mat) + _dot(a3, mat)


def _page_suffix(lf_rows, mat):
    n = lf_rows.shape[0]
    return pl.pallas_call(
        _page_suffix_kernel,
        out_shape=jax.ShapeDtypeStruct((n, 2 * PAGE), F32),
        grid=(n // PP_ROWS,),
        in_specs=[pl.BlockSpec((PP_ROWS, PAGE), lambda i: (i, 0)),
                  pl.BlockSpec((PAGE, 2 * PAGE), lambda i: (0, 0))],
        out_specs=pl.BlockSpec((PP_ROWS, 2 * PAGE), lambda i: (i, 0)),
        compiler_params=pltpu.CompilerParams(
            dimension_semantics=("arbitrary",), vmem_limit_bytes=VMEM_LIMIT_BYTES),
        name="page_suffix",
    )(lf_rows, mat)


def _layer(xp, xs, cache_k, cache_v, cache_logf, page_table, g_norm, w_in, b_f, g_q, g_k, g_v,
           w_s, b_s, w_out):
    B, S, _ = xp.shape
    nb = xs.shape[0]
    n_phys = cache_k.shape[0]

    cols = lambda a, n: w_in[:, a:a + n]
    wf = cols(_F0, N_HEADS)
    w = jnp.concatenate([cols(_K0, D_ATT), cols(_ZA0, D_ATT), cols(_U0, D_CHK), cols(_GV0, D_CHK),
                         cols(_ZC0, D_CHK)], axis=1).astype(BF16)
    pad8 = jnp.zeros((D_MODEL, 8), F32)
    wt = jnp.concatenate([cols(_Q0, D_ATT), cols(_V0, D_ATT), wf, pad8], axis=1).T.astype(BF16)
    wt_s = jnp.concatenate([cols(_Q0, D_ATT), cols(_K0, D_ATT), cols(_V0, D_ATT), wf, pad8],
                           axis=1).T.astype(BF16)
    wo = w_out.astype(BF16)
    gn = g_norm.reshape(1, D_MODEL)
    bft = b_f.reshape(N_HEADS, 1)
    gq = jnp.tile(g_q, N_HEADS).reshape(1, D_ATT)
    gqt = gq.reshape(D_ATT, 1)
    gk = jnp.tile(g_k, N_HEADS).reshape(1, D_ATT)
    gkt = gk.reshape(D_ATT, 1)
    gv = g_v.reshape(1, D_CHK)
    bsb = jnp.broadcast_to(b_s[:, :, None], (N_GROUPS, CHUNK, CH_GROUP))
    w00 = jnp.repeat(w_s[:, 0, 0], CH_GROUP).reshape(1, D_CHK)
    b0 = jnp.repeat(b_s[:, 0], CH_GROUP).reshape(1, D_CHK)
    utri = jnp.asarray(np.triu(np.ones((TM, TM), np.float32)), BF16)

    k_p, vt_p, lft_p, kb, e, qt, ext, va, ga, m = _prompt_proj(
        xp, gn, w, wt, bft, gqt, gk, gv, w_s, bsb, utri)
    xs2 = xs.reshape(nb, D_MODEL)
    k_s, gvn_s, qts, kts, vts, lfs, ga_s, m_s = _sample_proj(
        xs2, gn, w, wt_s, bft, gqt, gkt, gk, gv, w00, b0)

    lf_rows = jnp.transpose(cache_logf, (0, 2, 1)).reshape(n_phys * N_HEADS, PAGE)
    kt_pages = jnp.transpose(cache_k, (0, 2, 3, 1)).reshape(n_phys, D_ATT, PAGE)
    vt_pages = jnp.transpose(cache_v, (0, 2, 3, 1)).reshape(n_phys, D_ATT, PAGE)
    pp = _page_suffix(lf_rows, jnp.asarray(_page_suffix_matrix(), BF16))
    pp = pp.reshape(n_phys, N_HEADS, 2 * PAGE)
    attn, attn_s_t = _attention(page_table, qt, ext, kb, e, va, qts.T, qts, kts, vts, lfs,
                                kt_pages, vt_pages, pp)

    y_p = _merge(xp, attn, ga, m, wo, TM)
    y_s = _merge(xs2.reshape(1, nb, D_MODEL), attn_s_t.T.reshape(1, nb, D_ATT).astype(BF16),
                 ga_s.reshape(1, nb, D_ATT), m_s.reshape(1, nb, D_CHK), wo, nb)

    v_p = jnp.transpose(vt_p.reshape(B, N_HEADS, HEAD_DIM, S), (0, 3, 1, 2))
    lf_p = jnp.transpose(lft_p, (0, 2, 1))
    return (y_p, y_s.reshape(nb, 1, D_MODEL),
            k_p.reshape(B, S, N_HEADS, HEAD_DIM), v_p, lf_p,
            k_s.reshape(nb, 1, N_HEADS, HEAD_DIM), vts.T.reshape(nb, 1, N_HEADS, HEAD_DIM),
            lfs.T.reshape(nb, 1, N_HEADS), gvn_s.reshape(nb, 1, N_GROUPS, CH_GROUP))


def kernel(x_prompt, x_sample, cache_k, cache_v, cache_logf, page_table, g_norm, w_in, b_f, g_q, g_k,
           g_v, w_s, b_s, w_out):
    depth = w_in.shape[0]
    assert x_sample.shape[1] == 1, "one new token per decode sequence"
    xp, xs = x_prompt, x_sample
    outs = []
    for l in range(depth):
        res = _layer(xp, xs, cache_k[l], cache_v[l], cache_logf[l], page_table, g_norm[l], w_in[l],
                     b_f[l], g_q[l], g_k[l], g_v[l], w_s[l], b_s[l], w_out[l])
        xp, xs = res[0], res[1]
        outs.append(res[2:])
    stacked = tuple(jnp.stack([o[i] for o in outs]) for i in range(7))
    return (xp, xs) + stacked
```

```python
import functools

import numpy as np
import jax
import jax.numpy as jnp
from jax import lax
from jax.experimental import pallas as pl
from jax.experimental.pallas import tpu as pltpu

F32 = jnp.float32
BF16 = jnp.bfloat16

D_MODEL = 1024
N_HEADS = 8
HEAD_DIM = 64
D_ATT = N_HEADS * HEAD_DIM
N_GROUPS = 4
CH_GROUP = 128
D_CHK = N_GROUPS * CH_GROUP
CHUNK = 128
PAGE = 128
EPS = 1e-6
SCALE = HEAD_DIM ** -0.5
LOG2E = 1.4426950408889634

LANES = 128
VMEM_LIMIT_BYTES = 56 * 1024 * 1024

TM = 512
TQ = 512
TK = 256
KS = TQ
NQB = 4
V_ROWS = 80
X_ROWS = 32
PP_ROWS = 2048

_Q0, _K0, _V0, _F0 = 0, D_ATT, 2 * D_ATT, 3 * D_ATT
_ZA0 = 3 * D_ATT + N_HEADS
_U0 = _ZA0 + D_ATT
_GV0 = _U0 + D_CHK
_ZC0 = _GV0 + D_CHK
W_K, W_ZA, W_U, W_GV, W_ZC = (i * 512 for i in range(5))
W_COLS = 5 * 512
WT_ROWS = 2 * D_ATT + 16


def _log_sigmoid(x):
    return jnp.minimum(x, 0.0) - jnp.log1p(jnp.exp(-jnp.abs(x)))


def _silu(x):
    return x / (1.0 + jnp.exp(-x))


def _rms_rows(x, gain):
    ms = jnp.sum(x * x, axis=-1, keepdims=True) * (1.0 / x.shape[-1])
    return x * lax.rsqrt(ms + EPS) * gain


def _rms_head_pairs(x, gain):
    outs = []
    for p in range(x.shape[-1] // LANES):
        xp = x[:, p * LANES:(p + 1) * LANES]
        sq = xp * xp
        lo = lax.broadcasted_iota(jnp.int32, xp.shape, 1) < HEAD_DIM
        s_lo = jnp.sum(jnp.where(lo, sq, 0.0), axis=-1, keepdims=True)
        s_hi = jnp.sum(jnp.where(lo, 0.0, sq), axis=-1, keepdims=True)
        r = jnp.where(lo, lax.rsqrt(s_lo * (1.0 / HEAD_DIM) + EPS),
                      lax.rsqrt(s_hi * (1.0 / HEAD_DIM) + EPS))
        outs.append(xp * r * gain[:, p * LANES:(p + 1) * LANES])
    return jnp.concatenate(outs, axis=-1)


def _rms_groups(x, gain):
    outs = []
    for g in range(N_GROUPS):
        xg = x[:, g * CH_GROUP:(g + 1) * CH_GROUP]
        outs.append(_rms_rows(xg, gain[:, g * CH_GROUP:(g + 1) * CH_GROUP]))
    return jnp.concatenate(outs, axis=-1)


def _split3(a):
    a1 = a.astype(BF16)
    r1 = a - a1.astype(F32)
    a2 = r1.astype(BF16)
    a3 = (r1 - a2.astype(F32)).astype(BF16)
    return a1, a2, a3


def _dot(a, b):
    return jnp.dot(a, b, preferred_element_type=F32)


def _dot_nt(a, b):
    return lax.dot_general(a, b, (((1,), (1,)), ((), ())), preferred_element_type=F32)


def _prompt_proj_kernel(x_ref, gn_ref, w_ref, wt_ref, bft_ref, gqt_ref, gk_ref, gv_ref,
                        ws_ref, bsb_ref, utri_ref,
                        k_out, vt_out, lft_out, kb_out, e_out, qt_out, ext_out, va_out, ga_out, m_out,
                        carry_t):
    s_idx = pl.program_id(1)

    @pl.when(s_idx == 0)
    def _():
        carry_t[...] = jnp.zeros_like(carry_t)

    x = x_ref[0]
    h = _rms_rows(x, gn_ref[...]).astype(BF16)

    k = _rms_head_pairs(_dot(h, w_ref[:, W_K:W_K + D_ATT]), gk_ref[...])
    k_out[0] = k
    kb_out[0] = k.astype(BF16)

    t = _dot_nt(wt_ref[...], h)
    qt_out[0] = (_rms_heads_t(t[0:D_ATT], gqt_ref[...]) * (SCALE * LOG2E)).astype(BF16)

    vt = t[D_ATT:2 * D_ATT]
    vt_out[0] = vt
    row16 = lax.broadcasted_iota(jnp.int32, (V_ROWS - HEAD_DIM, TK), 0)
    tail = jnp.where(row16 == 0, 1.0, 0.0)
    for hh in range(N_HEADS):
        for kb in range(TM // TK):
            blk = vt[hh * HEAD_DIM:(hh + 1) * HEAD_DIM, kb * TK:(kb + 1) * TK]
            va_out[0, hh, kb] = jnp.concatenate([blk, tail], axis=0).astype(BF16)

    lft = _log_sigmoid(t[2 * D_ATT:2 * D_ATT + N_HEADS] + bft_ref[...])
    lft_out[0] = lft
    t1, t2, t3 = _split3(lft)
    utri = utri_ref[...]
    ct = carry_t[:, 0:1] + (_dot(t1, utri) + _dot(t2, utri) + _dot(t3, utri))
    carry_t[...] = jnp.broadcast_to(ct[:, TM - 1:TM], carry_t.shape)
    ct1, ct2, ct3 = (piece.astype(F32) for piece in _split3(ct * LOG2E))
    rid = lax.broadcasted_iota(jnp.int32, (X_ROWS, TM), 0)
    for hh in range(N_HEADS):
        sel = jnp.where(rid < 24, jnp.where((rid & 7) == hh, 1.0, 0.0), 0.0)
        ext = jnp.where(rid == 24, ct1[hh:hh + 1],
                        jnp.where(rid == 25, ct2[hh:hh + 1], jnp.where(rid == 26, ct3[hh:hh + 1], sel)))
        ext_out[0, hh] = ext.astype(BF16)
    rid8 = lax.broadcasted_iota(jnp.int32, (N_HEADS, TM), 0)
    ones3 = jnp.where(rid8 < 3, 1.0, 0.0)
    e_t = jnp.concatenate([-ct1, -ct2, -ct3, ones3, jnp.zeros((LANES - 4 * N_HEADS, TM), F32)], axis=0)
    e_out[0] = e_t.T.astype(BF16)

    ga_out[0] = _silu(_dot(h, w_ref[:, W_ZA:W_ZA + D_ATT])).astype(BF16)
    u = _dot(h, w_ref[:, W_U:W_U + D_CHK])
    gvn = _rms_groups(_dot(h, w_ref[:, W_GV:W_GV + D_CHK]), gv_ref[...]).astype(BF16)
    gate_c = _silu(_dot(h, w_ref[:, W_ZC:W_ZC + D_CHK]))
    tri = (lax.broadcasted_iota(jnp.int32, (CHUNK, CHUNK), 1)
           <= lax.broadcasted_iota(jnp.int32, (CHUNK, CHUNK), 0))
    for g in range(N_GROUPS):
        wg = jnp.where(tri, ws_ref[g], 0.0).astype(BF16)
        cs = slice(g * CH_GROUP, (g + 1) * CH_GROUP)
        for cc in range(TM // CHUNK):
            rs = slice(cc * CHUNK, (cc + 1) * CHUNK)
            mix = _dot(wg, gvn[rs, cs]) + bsb_ref[g]
            m_out[0, rs, cs] = (u[rs, cs] * mix * gate_c[rs, cs]).astype(BF16)


def _prompt_proj(x, gn, w, wt, bft, gqt, gk, gv, ws, bsb, utri):
    B, S, _ = x.shape
    nk = S // TK
    const2 = lambda shape: pl.BlockSpec(shape, lambda b, s: (0, 0))
    const3 = lambda shape: pl.BlockSpec(shape, lambda b, s: (0, 0, 0))
    row_blk = lambda width: pl.BlockSpec((1, TM, width), lambda b, s: (b, s, 0))
    col_blk = lambda rows: pl.BlockSpec((1, rows, TM), lambda b, s: (b, 0, s))
    out_shape = (
        jax.ShapeDtypeStruct((B, S, D_ATT), F32),
        jax.ShapeDtypeStruct((B, D_ATT, S), F32),
        jax.ShapeDtypeStruct((B, N_HEADS, S), F32),
        jax.ShapeDtypeStruct((B, S, D_ATT), BF16),
        jax.ShapeDtypeStruct((B, S, LANES), BF16),
        jax.ShapeDtypeStruct((B, D_ATT, S), BF16),
        jax.ShapeDtypeStruct((B, N_HEADS, X_ROWS, S), BF16),
        jax.ShapeDtypeStruct((B, N_HEADS, nk, V_ROWS, TK), BF16),
        jax.ShapeDtypeStruct((B, S, D_ATT), BF16),
        jax.ShapeDtypeStruct((B, S, D_CHK), BF16),
    )
    out_specs = (
        row_blk(D_ATT), col_blk(D_ATT), col_blk(N_HEADS), row_blk(D_ATT), row_blk(LANES),
        col_blk(D_ATT),
        pl.BlockSpec((1, N_HEADS, X_ROWS, TM), lambda b, s: (b, 0, 0, s)),
        pl.BlockSpec((1, N_HEADS, TM // TK, V_ROWS, TK), lambda b, s: (b, 0, s, 0, 0)),
        row_blk(D_ATT), row_blk(D_CHK),
    )
    in_specs = [
        row_blk(D_MODEL),
        const2((1, D_MODEL)),
        const2((D_MODEL, W_COLS)),
        const2((WT_ROWS, D_MODEL)),
        const2((N_HEADS, 1)),
        const2((D_ATT, 1)),
        const2((1, D_ATT)),
        const2((1, D_CHK)),
        const3((N_GROUPS, CHUNK, CHUNK)),
        const3((N_GROUPS, CHUNK, CH_GROUP)),
        const2((TM, TM)),
    ]
    return pl.pallas_call(
        _prompt_proj_kernel,
        out_shape=out_shape,
        grid=(B, S // TM),
        in_specs=in_specs,
        out_specs=out_specs,
        scratch_shapes=[pltpu.VMEM((N_HEADS, LANES), F32)],
        compiler_params=pltpu.CompilerParams(
            dimension_semantics=("arbitrary", "arbitrary"), vmem_limit_bytes=VMEM_LIMIT_BYTES),
        name="prompt_proj",
    )(x, gn, w, wt, bft, gqt, gk, gv, ws, bsb, utri)


def _head_sum(x):
    return jnp.sum(x.reshape(N_HEADS, HEAD_DIM, x.shape[-1]), axis=1)


def _per_dim(x):
    return jnp.broadcast_to(x[:, None, :], (N_HEADS, HEAD_DIM, x.shape[-1])).reshape(D_ATT, x.shape[-1])


def _page_copies(pt_ref, seq, slot, kt_hbm, vt_hbm, pp_hbm, kbuf, vbuf, ppbuf, sems):
    copies = []
    for j in range(kbuf.shape[1]):
        page = pt_ref[seq, j]
        copies.append(pltpu.make_async_copy(kt_hbm.at[page], kbuf.at[slot, j], sems.at[slot, 0]))
        copies.append(pltpu.make_async_copy(vt_hbm.at[page], vbuf.at[slot, j], sems.at[slot, 1]))
        copies.append(pltpu.make_async_copy(pp_hbm.at[page], ppbuf.at[slot, j], sems.at[slot, 2]))
    return copies


def _decode_past(seq, slot, q_ref, lfs_ref, kbuf, vbuf, ppbuf, os_ref, st_ref):
    n_pages = kbuf.shape[1]
    n_seq = os_ref.shape[-1]
    lane_is_seq = lambda rows: lax.broadcasted_iota(jnp.int32, (rows, n_seq), 1) == seq
    own = (lax.broadcasted_iota(jnp.int32, (N_HEADS, D_ATT), 0)
           == lax.broadcasted_iota(jnp.int32, (N_HEADS, D_ATT), 1) // HEAD_DIM)
    qbd = jnp.where(own, jnp.broadcast_to(q_ref[pl.ds(seq, 1), :], (N_HEADS, D_ATT)), 0.0)
    lf_new = jnp.sum(jnp.where(lane_is_seq(N_HEADS), lfs_ref[...], 0.0), axis=1, keepdims=True)

    off = jnp.broadcast_to(lf_new, (N_HEADS, PAGE))
    scores = [None] * n_pages
    for p in range(n_pages - 1, -1, -1):
        scores[p] = _dot(qbd, kbuf[slot, p]) + ppbuf[slot, p, :, 0:PAGE] + off
        off = off + ppbuf[slot, p, :, PAGE:2 * PAGE]
    top = scores[0]
    for p in range(1, n_pages):
        top = jnp.maximum(top, scores[p])
    m = jnp.max(top, axis=-1, keepdims=True)
    p_sum = jnp.zeros((N_HEADS, PAGE), F32)
    acc = jnp.zeros((D_ATT, PAGE), F32)
    for p in range(n_pages):
        pr = jnp.exp(scores[p] - m)
        p_sum = p_sum + pr
        acc = acc + vbuf[slot, p] * _per_dim(pr)
    os_ref[...] = jnp.where(lane_is_seq(D_ATT), jnp.sum(acc, axis=-1, keepdims=True), os_ref[...])
    stats = jnp.concatenate([m, jnp.sum(p_sum, axis=-1, keepdims=True)], axis=0)
    st_ref[...] = jnp.where(lane_is_seq(2 * N_HEADS), stats, st_ref[...])


def _decode_finish(qts_ref, kts_ref, vts_ref, os_ref, st_ref):
    s_self = _head_sum(qts_ref[...] * kts_ref[...])
    m_past, l_past = st_ref[0:N_HEADS, :], st_ref[N_HEADS:2 * N_HEADS, :]
    m = jnp.maximum(m_past, s_self)
    w_past, p_self = jnp.exp(m_past - m), jnp.exp(s_self - m)
    os_ref[...] = ((os_ref[...] * _per_dim(w_past) + _per_dim(p_self) * vts_ref[...])
                   / _per_dim(l_past * w_past + p_self))


def _query_block(i, qt, ext, kb_ref, e_ref, va_ref, wq_ref, s_ref, p_ref):
    last_tile = (i * TQ) // KS
    n_tiles = last_tile + 1

    wq_ref[...] = jnp.zeros_like(wq_ref)
    wq_ref[0:HEAD_DIM, 0:TQ] = qt[0:HEAD_DIM, :]
    wq_ref[HEAD_DIM:2 * HEAD_DIM, TQ:2 * TQ] = qt[HEAD_DIM:2 * HEAD_DIM, :]
    wq_ref[LANES:LANES + X_ROWS, 0:TQ] = ext[0]
    wq_ref[LANES:LANES + X_ROWS, TQ:2 * TQ] = ext[1]

    visit = lambda k: jnp.where(k == 0, last_tile, k - 1)

    def scores(tile):
        off = pl.multiple_of(tile * KS, KS)
        ke = jnp.concatenate([kb_ref[0, pl.ds(off, KS), :], e_ref[0, pl.ds(off, KS), :]], axis=1)
        s = _dot(ke, wq_ref[...])
        s_ref[...] = s
        return jnp.max(s, axis=0, keepdims=True)

    def probs(m, s, blk_max, valid=None):
        m_new = jnp.maximum(m, blk_max)
        alpha = jnp.exp2(m - m_new)
        if valid is not None:
            alpha = jnp.where(valid, alpha, 1.0)
            p_ref[...] = jnp.exp2(s - jnp.where(valid, m_new, jnp.inf)).astype(BF16)
            return jnp.where(valid, m_new, m), alpha
        p_ref[...] = jnp.exp2(s - m_new).astype(BF16)
        return m_new, alpha

    def accumulate(tile, alpha, accs):
        out = []
        for hh in range(2):
            cols = slice(hh * TQ, (hh + 1) * TQ)
            acc = alpha[:, cols] * accs[hh]
            for kk in range(KS // TK):
                acc = acc + _dot(va_ref[0, hh, tile * (KS // TK) + kk], p_ref[kk * TK:(kk + 1) * TK, cols])
            out.append(acc)
        return tuple(out)

    scores(last_tile)
    krow = lax.broadcasted_iota(jnp.int32, (KS, 2 * TQ), 0) + (last_tile * KS - i * TQ)
    qcol = lax.broadcasted_iota(jnp.int32, (KS, 2 * TQ), 1) & (TQ - 1)
    s_diag = jnp.where(krow <= qcol, s_ref[...], -jnp.inf)
    zero = jnp.zeros((V_ROWS, TQ), F32)
    m, alpha = probs(jnp.full((1, 2 * TQ), -jnp.inf, F32), s_diag, jnp.max(s_diag, axis=0, keepdims=True))
    state0 = (m, alpha, (zero, zero), scores(0))

    def body(k, carry):
        m, alpha, accs, blk_max = carry
        accs = accumulate(visit(k), alpha, accs)
        m, alpha = probs(m, s_ref[...], blk_max)
        return m, alpha, accs, scores(k + 1)

    def steady(state):
        return lax.fori_loop(0, jnp.maximum(n_tiles - 2, 0), body, state)

    def drain(state):
        m, alpha, accs, blk_max = state
        accs = accumulate(visit(jnp.maximum(n_tiles - 2, 0)), alpha, accs)
        m, alpha = probs(m, s_ref[...], blk_max, valid=n_tiles >= 2)
        acc0, acc1 = accumulate(visit(n_tiles - 1), alpha, accs)
        o0 = acc0[0:HEAD_DIM] / acc0[HEAD_DIM:HEAD_DIM + 1]
        o1 = acc1[0:HEAD_DIM] / acc1[HEAD_DIM:HEAD_DIM + 1]
        return jnp.concatenate([o0, o1], axis=0).T.astype(BF16)

    return state0, steady, drain


def _attention_kernel(n_seq, pt_ref, qt_ref, ext_ref, kb_ref, e_ref, va_ref,
                      qs_ref, qts_ref, kts_ref, vts_ref, lfs_ref, kt_hbm, vt_hbm, pp_hbm,
                      o_ref, os_ref, wq_refs, s_refs, p_refs, kbuf, vbuf, ppbuf, st_ref, sems):
    step = (pl.program_id(0) * pl.num_programs(1) + pl.program_id(1)) * pl.num_programs(2) + pl.program_id(2)
    copies = functools.partial(_page_copies, pt_ref, kt_hbm=kt_hbm, vt_hbm=vt_hbm, pp_hbm=pp_hbm,
                               kbuf=kbuf, vbuf=vbuf, ppbuf=ppbuf, sems=sems)

    @pl.when(step == 0)
    def _():
        os_ref[...] = jnp.zeros_like(os_ref)
        st_ref[...] = jnp.zeros_like(st_ref)
        for c in copies(0, 0):
            c.start()

    pending = None
    for j in range(NQB):
        seq = step * NQB + j
        buf = j % 2
        for c in copies(seq, buf):
            c.wait()
        for c in copies(jnp.minimum(seq + 1, n_seq - 1), 1 - buf):
            c.start()
        _decode_past(seq, buf, qs_ref, lfs_ref, kbuf, vbuf, ppbuf, os_ref, st_ref)
        if pending is not None:
            state, prev_drain, prev_rows = pending
            o_ref[0, prev_rows, :] = prev_drain(state)
        rows = slice(j * TQ, (j + 1) * TQ)
        state0, steady, drain = _query_block(
            pl.program_id(2) * NQB + j, qt_ref[0, :, rows], ext_ref[0, :, :, rows],
            kb_ref, e_ref, va_ref, wq_refs.at[buf], s_refs.at[buf], p_refs.at[buf])
        pending = (steady(state0), drain, rows)
    state, drain, rows = pending
    o_ref[0, rows, :] = drain(state)

    @pl.when(seq == n_seq - 1)
    def _():
        for c in copies(n_seq - 1, 1 - buf):
            c.wait()
        _decode_finish(qts_ref, kts_ref, vts_ref, os_ref, st_ref)


def _attention(page_table, qt, ext, kb, e, va, qs, qts, kts, vts, lfs, kt_pages, vt_pages, pp):
    B, _, S = qt.shape
    nk = S // TK
    n_seq, n_pages = page_table.shape
    tq = NQB * TQ
    grid = (B, N_HEADS // 2, S // tq)
    assert NQB % 2 == 0 and n_seq == grid[0] * grid[1] * grid[2] * NQB, "one decode sequence per query block"
    resident = lambda a: pl.BlockSpec(a.shape, lambda b, p, g, pt: (0, 0))
    in_hbm = pl.BlockSpec(memory_space=pl.ANY)
    in_specs = [
        pl.BlockSpec((1, LANES, tq), lambda b, p, g, pt: (b, p, g)),
        pl.BlockSpec((1, 2, X_ROWS, tq), lambda b, p, g, pt: (b, p, 0, g)),
        pl.BlockSpec((1, S, LANES), lambda b, p, g, pt: (b, 0, p)),
        pl.BlockSpec((1, S, LANES), lambda b, p, g, pt: (b, 0, 0)),
        pl.BlockSpec((1, 2, nk, V_ROWS, TK), lambda b, p, g, pt: (b, p, 0, 0, 0)),
        resident(qs), resident(qts), resident(kts), resident(vts), resident(lfs),
        in_hbm, in_hbm, in_hbm]
    grid_spec = pltpu.PrefetchScalarGridSpec(
        num_scalar_prefetch=1, grid=grid, in_specs=in_specs,
        out_specs=(pl.BlockSpec((1, tq, LANES), lambda b, p, g, pt: (b, g, p)), resident(qts)),
        scratch_shapes=[pltpu.VMEM((2, 2 * LANES, 2 * TQ), BF16), pltpu.VMEM((2, KS, 2 * TQ), F32),
                        pltpu.VMEM((2, KS, 2 * TQ), BF16),
                        pltpu.VMEM((2, n_pages, D_ATT, PAGE), F32), pltpu.VMEM((2, n_pages, D_ATT, PAGE), F32),
                        pltpu.VMEM((2, n_pages, N_HEADS, 2 * PAGE), F32),
                        pltpu.VMEM((2 * N_HEADS, n_seq), F32), pltpu.SemaphoreType.DMA((2, 3))])
    return pl.pallas_call(
        functools.partial(_attention_kernel, n_seq),
        out_shape=(jax.ShapeDtypeStruct((B, S, D_ATT), BF16), jax.ShapeDtypeStruct((D_ATT, n_seq), F32)),
        grid_spec=grid_spec,
        compiler_params=pltpu.CompilerParams(
            dimension_semantics=("arbitrary", "arbitrary", "arbitrary"),
            vmem_limit_bytes=VMEM_LIMIT_BYTES),
        name="attention",
    )(page_table, qt, ext, kb, e, va, qs, qts, kts, vts, lfs, kt_pages, vt_pages, pp)


def _merge_kernel(x_ref, a_ref, ga_ref, m_ref, wo_ref, y_ref):
    a = (a_ref[0].astype(F32) * ga_ref[0].astype(F32)).astype(BF16)
    y_ref[0] = (x_ref[0] + _dot(a, wo_ref[0:D_ATT, :]) + _dot(m_ref[0], wo_ref[D_ATT:D_ATT + D_CHK, :]))


def _merge(x, a, ga, m, wo, tm):
    B, S, _ = x.shape
    row_blk = lambda width: pl.BlockSpec((1, tm, width), lambda b, s: (b, s, 0))
    return pl.pallas_call(
        _merge_kernel,
        out_shape=jax.ShapeDtypeStruct((B, S, D_MODEL), F32),
        grid=(B, S // tm),
        in_specs=[row_blk(D_MODEL), row_blk(D_ATT), row_blk(D_ATT), row_blk(D_CHK),
                  pl.BlockSpec((D_ATT + D_CHK, D_MODEL), lambda b, s: (0, 0))],
        out_specs=row_blk(D_MODEL),
        compiler_params=pltpu.CompilerParams(
            dimension_semantics=("arbitrary", "arbitrary"), vmem_limit_bytes=VMEM_LIMIT_BYTES),
        name="merge",
    )(x, a, ga, m, wo)


def _rms_heads_t(xt, gain_col):
    n = xt.shape[-1]
    x3 = xt.reshape(N_HEADS, HEAD_DIM, n)
    ssq = jnp.sum(x3 * x3, axis=1, keepdims=True)
    return (x3 * lax.rsqrt(ssq * (1.0 / HEAD_DIM) + EPS)).reshape(D_ATT, n) * gain_col


def _sample_proj_kernel(x_ref, gn_ref, w_ref, wt_ref, bft_ref, gqt_ref, gkt_ref, gk_ref, gv_ref,
                        w00_ref, b0_ref,
                        k_out, gvn_out, qt_out, kt_out, vt_out, lft_out, ga_out, m_out):
    h = _rms_rows(x_ref[...], gn_ref[...]).astype(BF16)
    t = _dot_nt(wt_ref[...], h)
    qt_out[...] = _rms_heads_t(t[0:D_ATT], gqt_ref[...]) * SCALE
    kt_out[...] = _rms_heads_t(t[D_ATT:2 * D_ATT], gkt_ref[...])
    vt_out[...] = t[2 * D_ATT:3 * D_ATT]
    lft_out[...] = _log_sigmoid(t[3 * D_ATT:3 * D_ATT + N_HEADS] + bft_ref[...])
    k_out[...] = _rms_head_pairs(_dot(h, w_ref[:, W_K:W_K + D_ATT]), gk_ref[...])
    ga_out[...] = _silu(_dot(h, w_ref[:, W_ZA:W_ZA + D_ATT])).astype(BF16)
    u = _dot(h, w_ref[:, W_U:W_U + D_CHK])
    gvn = _rms_groups(_dot(h, w_ref[:, W_GV:W_GV + D_CHK]), gv_ref[...])
    gvn_out[...] = gvn
    gate_c = _silu(_dot(h, w_ref[:, W_ZC:W_ZC + D_CHK]))
    m_out[...] = (u * (w00_ref[...] * gvn + b0_ref[...]) * gate_c).astype(BF16)


def _sample_proj(x, gn, w, wt_s, bft, gqt, gkt, gk, gv, w00, b0):
    n = x.shape[0]
    full = lambda a: pl.BlockSpec(a.shape, lambda i: (0,) * a.ndim)
    args = (x, gn, w, wt_s, bft, gqt, gkt, gk, gv, w00, b0)
    out_shape = (
        jax.ShapeDtypeStruct((n, D_ATT), F32), jax.ShapeDtypeStruct((n, D_CHK), F32),
        jax.ShapeDtypeStruct((D_ATT, n), F32), jax.ShapeDtypeStruct((D_ATT, n), F32),
        jax.ShapeDtypeStruct((D_ATT, n), F32), jax.ShapeDtypeStruct((N_HEADS, n), F32),
        jax.ShapeDtypeStruct((n, D_ATT), BF16), jax.ShapeDtypeStruct((n, D_CHK), BF16),
    )
    return pl.pallas_call(
        _sample_proj_kernel,
        out_shape=out_shape,
        grid=(1,),
        in_specs=[full(a) for a in args],
        out_specs=tuple(pl.BlockSpec(o.shape, lambda i: (0, 0)) for o in out_shape),
        compiler_params=pltpu.CompilerParams(
            dimension_semantics=("arbitrary",), vmem_limit_bytes=VMEM_LIMIT_BYTES),
        name="sample_proj",
    )(*args)


def _page_suffix_matrix():
    pos = np.arange(PAGE)
    later = pos[:, None] > pos[None, :]
    return np.concatenate([later, np.ones((PAGE, PAGE), bool)], axis=1).astype(np.float32)


def _page_suffix_kernel(lf_ref, mat_ref, o_ref):
    a1, a2, a3 = _split3(lf_ref[...])
    mat = mat_ref[...]
    o_ref[...] = _dot(a1, mat) + _dot(a2, mat) + _dot(a3, mat)


def _page_suffix(lf_rows, mat):
    n = lf_rows.shape[0]
    return pl.pallas_call(
        _page_suffix_kernel,
        out_shape=jax.ShapeDtypeStruct((n, 2 * PAGE), F32),
        grid=(n // PP_ROWS,),
        in_specs=[pl.BlockSpec((PP_ROWS, PAGE), lambda i: (i, 0)),
                  pl.BlockSpec((PAGE, 2 * PAGE), lambda i: (0, 0))],
        out_specs=pl.BlockSpec((PP_ROWS, 2 * PAGE), lambda i: (i, 0)),
        compiler_params=pltpu.CompilerParams(
            dimension_semantics=("arbitrary",), vmem_limit_bytes=VMEM_LIMIT_BYTES),
        name="page_suffix",
    )(lf_rows, mat)


def _layer(xp, xs, cache_k, cache_v, cache_logf, page_table, g_norm, w_in, b_f, g_q, g_k, g_v,
           w_s, b_s, w_out):
    B, S, _ = xp.shape
    nb = xs.shape[0]
    n_phys = cache_k.shape[0]

    cols = lambda a, n: w_in[:, a:a + n]
    wf = cols(_F0, N_HEADS)
    w = jnp.concatenate([cols(_K0, D_ATT), cols(_ZA0, D_ATT), cols(_U0, D_CHK), cols(_GV0, D_CHK),
                         cols(_ZC0, D_CHK)], axis=1).astype(BF16)
    pad8 = jnp.zeros((D_MODEL, 8), F32)
    wt = jnp.concatenate([cols(_Q0, D_ATT), cols(_V0, D_ATT), wf, pad8], axis=1).T.astype(BF16)
    wt_s = jnp.concatenate([cols(_Q0, D_ATT), cols(_K0, D_ATT), cols(_V0, D_ATT), wf, pad8],
                           axis=1).T.astype(BF16)
    wo = w_out.astype(BF16)
    gn = g_norm.reshape(1, D_MODEL)
    bft = b_f.reshape(N_HEADS, 1)
    gq = jnp.tile(g_q, N_HEADS).reshape(1, D_ATT)
    gqt = gq.reshape(D_ATT, 1)
    gk = jnp.tile(g_k, N_HEADS).reshape(1, D_ATT)
    gkt = gk.reshape(D_ATT, 1)
    gv = g_v.reshape(1, D_CHK)
    bsb = jnp.broadcast_to(b_s[:, :, None], (N_GROUPS, CHUNK, CH_GROUP))
    w00 = jnp.repeat(w_s[:, 0, 0], CH_GROUP).reshape(1, D_CHK)
    b0 = jnp.repeat(b_s[:, 0], CH_GROUP).reshape(1, D_CHK)
    utri = jnp.asarray(np.triu(np.ones((TM, TM), np.float32)), BF16)

    k_p, vt_p, lft_p, kb, e, qt, ext, va, ga, m = _prompt_proj(
        xp, gn, w, wt, bft, gqt, gk, gv, w_s, bsb, utri)
    xs2 = xs.reshape(nb, D_MODEL)
    k_s, gvn_s, qts, kts, vts, lfs, ga_s, m_s = _sample_proj(
        xs2, gn, w, wt_s, bft, gqt, gkt, gk, gv, w00, b0)

    lf_rows = jnp.transpose(cache_logf, (0, 2, 1)).reshape(n_phys * N_HEADS, PAGE)
    kt_pages = jnp.transpose(cache_k, (0, 2, 3, 1)).reshape(n_phys, D_ATT, PAGE)
    vt_pages = jnp.transpose(cache_v, (0, 2, 3, 1)).reshape(n_phys, D_ATT, PAGE)
    pp = _page_suffix(lf_rows, jnp.asarray(_page_suffix_matrix(), BF16))
    pp = pp.reshape(n_phys, N_HEADS, 2 * PAGE)
    attn, attn_s_t = _attention(page_table, qt, ext, kb, e, va, qts.T, qts, kts, vts, lfs,
                                kt_pages, vt_pages, pp)

    y_p = _merge(xp, attn, ga, m, wo, 2 * TM)
    y_s = _merge(xs2.reshape(1, nb, D_MODEL), attn_s_t.T.reshape(1, nb, D_ATT).astype(BF16),
                 ga_s.reshape(1, nb, D_ATT), m_s.reshape(1, nb, D_CHK), wo, nb)

    v_p = jnp.transpose(vt_p.reshape(B, N_HEADS, HEAD_DIM, S), (0, 3, 1, 2))
    lf_p = jnp.transpose(lft_p, (0, 2, 1))
    return (y_p, y_s.reshape(nb, 1, D_MODEL),
            k_p.reshape(B, S, N_HEADS, HEAD_DIM), v_p, lf_p,
            k_s.reshape(nb, 1, N_HEADS, HEAD_DIM), vts.T.reshape(nb, 1, N_HEADS, HEAD_DIM),
            lfs.T.reshape(nb, 1, N_HEADS), gvn_s.reshape(nb, 1, N_GROUPS, CH_GROUP))


def kernel(x_prompt, x_sample, cache_k, cache_v, cache_logf, page_table, g_norm, w_in, b_f, g_q, g_k,
           g_v, w_s, b_s, w_out):
    depth = w_in.shape[0]
    assert x_sample.shape[1] == 1, "one new token per decode sequence"
    xp, xs = x_prompt, x_sample
    outs = []
    for l in range(depth):
        res = _layer(xp, xs, cache_k[l], cache_v[l], cache_logf[l], page_table, g_norm[l], w_in[l],
                     b_f[l], g_q[l], g_k[l], g_v[l], w_s[l], b_s[l], w_out[l])
        xp, xs = res[0], res[1]
        outs.append(res[2:])
    stacked = tuple(jnp.stack([o[i] for o in outs]) for i in range(7))
    return (xp, xs) + stacked
```

```python
import functools

import numpy as np
import jax
import jax.numpy as jnp
from jax import lax
from jax.experimental import pallas as pl
from jax.experimental.pallas import tpu as pltpu

F32 = jnp.float32
BF16 = jnp.bfloat16

D_MODEL = 1024
N_HEADS = 8
HEAD_DIM = 64
D_ATT = N_HEADS * HEAD_DIM
N_GROUPS = 4
CH_GROUP = 128
D_CHK = N_GROUPS * CH_GROUP
CHUNK = 128
PAGE = 128
EPS = 1e-6
SCALE = HEAD_DIM ** -0.5
LOG2E = 1.4426950408889634

LANES = 128
VMEM_LIMIT_BYTES = 56 * 1024 * 1024

TM = 512
TQ = 512
TK = 256
KS = TQ
NQB = 4
V_ROWS = 80
X_ROWS = 32
PP_ROWS = 2048

_Q0, _K0, _V0, _F0 = 0, D_ATT, 2 * D_ATT, 3 * D_ATT
_ZA0 = 3 * D_ATT + N_HEADS
_U0 = _ZA0 + D_ATT
_GV0 = _U0 + D_CHK
_ZC0 = _GV0 + D_CHK
W_K, W_ZA, W_U, W_GV, W_ZC = (i * 512 for i in range(5))
W_COLS = 5 * 512
WT_ROWS = 2 * D_ATT + 16


def _log_sigmoid(x):
    return jnp.minimum(x, 0.0) - jnp.log1p(jnp.exp(-jnp.abs(x)))


def _silu(x):
    return x / (1.0 + jnp.exp(-x))


def _rms_rows(x, gain):
    ms = jnp.sum(x * x, axis=-1, keepdims=True) * (1.0 / x.shape[-1])
    return x * lax.rsqrt(ms + EPS) * gain


def _rms_head_pairs(x, gain):
    outs = []
    for p in range(x.shape[-1] // LANES):
        xp = x[:, p * LANES:(p + 1) * LANES]
        sq = xp * xp
        lo = lax.broadcasted_iota(jnp.int32, xp.shape, 1) < HEAD_DIM
        s_lo = jnp.sum(jnp.where(lo, sq, 0.0), axis=-1, keepdims=True)
        s_hi = jnp.sum(jnp.where(lo, 0.0, sq), axis=-1, keepdims=True)
        r = jnp.where(lo, lax.rsqrt(s_lo * (1.0 / HEAD_DIM) + EPS),
                      lax.rsqrt(s_hi * (1.0 / HEAD_DIM) + EPS))
        outs.append(xp * r * gain[:, p * LANES:(p + 1) * LANES])
    return jnp.concatenate(outs, axis=-1)


def _rms_groups(x, gain):
    outs = []
    for g in range(N_GROUPS):
        xg = x[:, g * CH_GROUP:(g + 1) * CH_GROUP]
        outs.append(_rms_rows(xg, gain[:, g * CH_GROUP:(g + 1) * CH_GROUP]))
    return jnp.concatenate(outs, axis=-1)


def _split3(a):
    a1 = a.astype(BF16)
    r1 = a - a1.astype(F32)
    a2 = r1.astype(BF16)
    a3 = (r1 - a2.astype(F32)).astype(BF16)
    return a1, a2, a3


def _dot(a, b):
    return jnp.dot(a, b, preferred_element_type=F32)


def _dot_nt(a, b):
    return lax.dot_general(a, b, (((1,), (1,)), ((), ())), preferred_element_type=F32)


def _prompt_proj_kernel(x_ref, gn_ref, w_ref, wt_ref, bft_ref, gqt_ref, gk_ref, gv_ref,
                        ws_ref, bsb_ref, utri_ref,
                        k_out, vt_out, lft_out, kb_out, e_out, qt_out, ext_out, va_out, ga_out, m_out,
                        carry_t):
    s_idx = pl.program_id(1)

    @pl.when(s_idx == 0)
    def _():
        carry_t[...] = jnp.zeros_like(carry_t)

    x = x_ref[0]
    h = _rms_rows(x, gn_ref[...]).astype(BF16)

    k = _rms_head_pairs(_dot(h, w_ref[:, W_K:W_K + D_ATT]), gk_ref[...])
    k_out[0] = k
    kb_out[0] = k.astype(BF16)

    t = _dot_nt(wt_ref[...], h)
    qt_out[0] = (_rms_heads_t(t[0:D_ATT], gqt_ref[...]) * (SCALE * LOG2E)).astype(BF16)

    vt = t[D_ATT:2 * D_ATT]
    vt_out[0] = vt
    row16 = lax.broadcasted_iota(jnp.int32, (V_ROWS - HEAD_DIM, TK), 0)
    tail = jnp.where(row16 == 0, 1.0, 0.0)
    for hh in range(N_HEADS):
        for kb in range(TM // TK):
            blk = vt[hh * HEAD_DIM:(hh + 1) * HEAD_DIM, kb * TK:(kb + 1) * TK]
            va_out[0, hh, kb] = jnp.concatenate([blk, tail], axis=0).astype(BF16)

    lft = _log_sigmoid(t[2 * D_ATT:2 * D_ATT + N_HEADS] + bft_ref[...])
    lft_out[0] = lft
    t1, t2, t3 = _split3(lft)
    utri = utri_ref[...]
    ct = carry_t[:, 0:1] + (_dot(t1, utri) + _dot(t2, utri) + _dot(t3, utri))
    carry_t[...] = jnp.broadcast_to(ct[:, TM - 1:TM], carry_t.shape)
    ct1, ct2, ct3 = (piece.astype(F32) for piece in _split3(ct * LOG2E))
    rid = lax.broadcasted_iota(jnp.int32, (X_ROWS, TM), 0)
    for hh in range(N_HEADS):
        sel = jnp.where(rid < 24, jnp.where((rid & 7) == hh, 1.0, 0.0), 0.0)
        ext = jnp.where(rid == 24, ct1[hh:hh + 1],
                        jnp.where(rid == 25, ct2[hh:hh + 1], jnp.where(rid == 26, ct3[hh:hh + 1], sel)))
        ext_out[0, hh] = ext.astype(BF16)
    rid8 = lax.broadcasted_iota(jnp.int32, (N_HEADS, TM), 0)
    ones3 = jnp.where(rid8 < 3, 1.0, 0.0)
    e_t = jnp.concatenate([-ct1, -ct2, -ct3, ones3, jnp.zeros((LANES - 4 * N_HEADS, TM), F32)], axis=0)
    e_out[0] = e_t.T.astype(BF16)

    ga_out[0] = _silu(_dot(h, w_ref[:, W_ZA:W_ZA + D_ATT])).astype(BF16)
    u = _dot(h, w_ref[:, W_U:W_U + D_CHK])
    gvn = _rms_groups(_dot(h, w_ref[:, W_GV:W_GV + D_CHK]), gv_ref[...]).astype(BF16)
    gate_c = _silu(_dot(h, w_ref[:, W_ZC:W_ZC + D_CHK]))
    tri = (lax.broadcasted_iota(jnp.int32, (CHUNK, CHUNK), 1)
           <= lax.broadcasted_iota(jnp.int32, (CHUNK, CHUNK), 0))
    for g in range(N_GROUPS):
        wg = jnp.where(tri, ws_ref[g], 0.0).astype(BF16)
        cs = slice(g * CH_GROUP, (g + 1) * CH_GROUP)
        for cc in range(TM // CHUNK):
            rs = slice(cc * CHUNK, (cc + 1) * CHUNK)
            mix = _dot(wg, gvn[rs, cs]) + bsb_ref[g]
            m_out[0, rs, cs] = (u[rs, cs] * mix * gate_c[rs, cs]).astype(BF16)


def _prompt_proj(x, gn, w, wt, bft, gqt, gk, gv, ws, bsb, utri):
    B, S, _ = x.shape
    nk = S // TK
    const2 = lambda shape: pl.BlockSpec(shape, lambda b, s: (0, 0))
    const3 = lambda shape: pl.BlockSpec(shape, lambda b, s: (0, 0, 0))
    row_blk = lambda width: pl.BlockSpec((1, TM, width), lambda b, s: (b, s, 0))
    col_blk = lambda rows: pl.BlockSpec((1, rows, TM), lambda b, s: (b, 0, s))
    out_shape = (
        jax.ShapeDtypeStruct((B, S, D_ATT), F32),
        jax.ShapeDtypeStruct((B, D_ATT, S), F32),
        jax.ShapeDtypeStruct((B, N_HEADS, S), F32),
        jax.ShapeDtypeStruct((B, S, D_ATT), BF16),
        jax.ShapeDtypeStruct((B, S, LANES), BF16),
        jax.ShapeDtypeStruct((B, D_ATT, S), BF16),
        jax.ShapeDtypeStruct((B, N_HEADS, X_ROWS, S), BF16),
        jax.ShapeDtypeStruct((B, N_HEADS, nk, V_ROWS, TK), BF16),
        jax.ShapeDtypeStruct((B, S, D_ATT), BF16),
        jax.ShapeDtypeStruct((B, S, D_CHK), BF16),
    )
    out_specs = (
        row_blk(D_ATT), col_blk(D_ATT), col_blk(N_HEADS), row_blk(D_ATT), row_blk(LANES),
        col_blk(D_ATT),
        pl.BlockSpec((1, N_HEADS, X_ROWS, TM), lambda b, s: (b, 0, 0, s)),
        pl.BlockSpec((1, N_HEADS, TM // TK, V_ROWS, TK), lambda b, s: (b, 0, s, 0, 0)),
        row_blk(D_ATT), row_blk(D_CHK),
    )
    in_specs = [
        row_blk(D_MODEL),
        const2((1, D_MODEL)),
        const2((D_MODEL, W_COLS)),
        const2((WT_ROWS, D_MODEL)),
        const2((N_HEADS, 1)),
        const2((D_ATT, 1)),
        const2((1, D_ATT)),
        const2((1, D_CHK)),
        const3((N_GROUPS, CHUNK, CHUNK)),
        const3((N_GROUPS, CHUNK, CH_GROUP)),
        const2((TM, TM)),
    ]
    return pl.pallas_call(
        _prompt_proj_kernel,
        out_shape=out_shape,
        grid=(B, S // TM),
        in_specs=in_specs,
        out_specs=out_specs,
        scratch_shapes=[pltpu.VMEM((N_HEADS, LANES), F32)],
        compiler_params=pltpu.CompilerParams(
            dimension_semantics=("arbitrary", "arbitrary"), vmem_limit_bytes=VMEM_LIMIT_BYTES),
        name="prompt_proj",
    )(x, gn, w, wt, bft, gqt, gk, gv, ws, bsb, utri)


def _head_sum(x):
    return jnp.sum(x.reshape(N_HEADS, HEAD_DIM, x.shape[-1]), axis=1)


def _per_dim(x):
    return jnp.broadcast_to(x[:, None, :], (N_HEADS, HEAD_DIM, x.shape[-1])).reshape(D_ATT, x.shape[-1])


def _page_copies(pt_ref, seq, slot, kt_hbm, vt_hbm, pp_hbm, kbuf, vbuf, ppbuf, sems):
    copies = []
    for j in range(kbuf.shape[1]):
        page = pt_ref[seq, j]
        copies.append(pltpu.make_async_copy(kt_hbm.at[page], kbuf.at[slot, j], sems.at[slot, 0]))
        copies.append(pltpu.make_async_copy(vt_hbm.at[page], vbuf.at[slot, j], sems.at[slot, 1]))
        copies.append(pltpu.make_async_copy(pp_hbm.at[page], ppbuf.at[slot, j], sems.at[slot, 2]))
    return copies


def _decode_past(seq, slot, q_ref, lfs_ref, kbuf, vbuf, ppbuf, os_ref, st_ref):
    n_pages = kbuf.shape[1]
    n_seq = os_ref.shape[-1]
    lane_is_seq = lambda rows: lax.broadcasted_iota(jnp.int32, (rows, n_seq), 1) == seq
    own = (lax.broadcasted_iota(jnp.int32, (N_HEADS, D_ATT), 0)
           == lax.broadcasted_iota(jnp.int32, (N_HEADS, D_ATT), 1) // HEAD_DIM)
    qbd = jnp.where(own, jnp.broadcast_to(q_ref[pl.ds(seq, 1), :], (N_HEADS, D_ATT)), 0.0)
    lf_new = jnp.sum(jnp.where(lane_is_seq(N_HEADS), lfs_ref[...], 0.0), axis=1, keepdims=True)

    off = jnp.broadcast_to(lf_new, (N_HEADS, PAGE))
    scores = [None] * n_pages
    for p in range(n_pages - 1, -1, -1):
        scores[p] = _dot(qbd, kbuf[slot, p]) + ppbuf[slot, p, :, 0:PAGE] + off
        off = off + ppbuf[slot, p, :, PAGE:2 * PAGE]
    top = scores[0]
    for p in range(1, n_pages):
        top = jnp.maximum(top, scores[p])
    m = jnp.max(top, axis=-1, keepdims=True)
    probs = [jnp.exp(s - m) for s in scores]
    p_sum = probs[0]
    for p in range(1, n_pages):
        p_sum = p_sum + probs[p]
    cols = []
    for hh in range(N_HEADS):
        rows = slice(hh * HEAD_DIM, (hh + 1) * HEAD_DIM)
        acc = vbuf[slot, 0, rows, :] * probs[0][hh:hh + 1, :]
        for p in range(1, n_pages):
            acc = acc + vbuf[slot, p, rows, :] * probs[p][hh:hh + 1, :]
        cols.append(jnp.sum(acc, axis=-1, keepdims=True))
    os_ref[...] = jnp.where(lane_is_seq(D_ATT), jnp.concatenate(cols, axis=0), os_ref[...])
    stats = jnp.concatenate([m, jnp.sum(p_sum, axis=-1, keepdims=True)], axis=0)
    st_ref[...] = jnp.where(lane_is_seq(2 * N_HEADS), stats, st_ref[...])


def _decode_finish(qts_ref, kts_ref, vts_ref, os_ref, st_ref):
    s_self = _head_sum(qts_ref[...] * kts_ref[...])
    m_past, l_past = st_ref[0:N_HEADS, :], st_ref[N_HEADS:2 * N_HEADS, :]
    m = jnp.maximum(m_past, s_self)
    w_past, p_self = jnp.exp(m_past - m), jnp.exp(s_self - m)
    os_ref[...] = ((os_ref[...] * _per_dim(w_past) + _per_dim(p_self) * vts_ref[...])
                   / _per_dim(l_past * w_past + p_self))


def _query_block(i, qt, ext, kb_ref, e_ref, va_ref, wq_ref, s_ref, p_ref):
    last_tile = (i * TQ) // KS
    n_tiles = last_tile + 1

    wq_ref[...] = jnp.zeros_like(wq_ref)
    wq_ref[0:HEAD_DIM, 0:TQ] = qt[0:HEAD_DIM, :]
    wq_ref[HEAD_DIM:2 * HEAD_DIM, TQ:2 * TQ] = qt[HEAD_DIM:2 * HEAD_DIM, :]
    wq_ref[LANES:LANES + X_ROWS, 0:TQ] = ext[0]
    wq_ref[LANES:LANES + X_ROWS, TQ:2 * TQ] = ext[1]

    visit = lambda k: jnp.where(k == 0, last_tile, k - 1)

    def scores(tile):
        off = pl.multiple_of(tile * KS, KS)
        ke = jnp.concatenate([kb_ref[0, pl.ds(off, KS), :], e_ref[0, pl.ds(off, KS), :]], axis=1)
        s = _dot(ke, wq_ref[...])
        s_ref[...] = s
        return jnp.max(s, axis=0, keepdims=True)

    def probs(m, s, blk_max, valid=None):
        m_new = jnp.maximum(m, blk_max)
        alpha = jnp.exp2(m - m_new)
        if valid is not None:
            alpha = jnp.where(valid, alpha, 1.0)
            p_ref[...] = jnp.exp2(s - jnp.where(valid, m_new, jnp.inf)).astype(BF16)
            return jnp.where(valid, m_new, m), alpha
        p_ref[...] = jnp.exp2(s - m_new).astype(BF16)
        return m_new, alpha

    def accumulate(tile, alpha, accs):
        out = []
        for hh in range(2):
            cols = slice(hh * TQ, (hh + 1) * TQ)
            acc = alpha[:, cols] * accs[hh]
            for kk in range(KS // TK):
                acc = acc + _dot(va_ref[0, hh, tile * (KS // TK) + kk], p_ref[kk * TK:(kk + 1) * TK, cols])
            out.append(acc)
        return tuple(out)

    scores(last_tile)
    krow = lax.broadcasted_iota(jnp.int32, (KS, 2 * TQ), 0) + (last_tile * KS - i * TQ)
    qcol = lax.broadcasted_iota(jnp.int32, (KS, 2 * TQ), 1) & (TQ - 1)
    s_diag = jnp.where(krow <= qcol, s_ref[...], -jnp.inf)
    zero = jnp.zeros((V_ROWS, TQ), F32)
    m, alpha = probs(jnp.full((1, 2 * TQ), -jnp.inf, F32), s_diag, jnp.max(s_diag, axis=0, keepdims=True))
    state0 = (m, alpha, (zero, zero), scores(0))

    def body(k, carry):
        m, alpha, accs, blk_max = carry
        accs = accumulate(visit(k), alpha, accs)
        m, alpha = probs(m, s_ref[...], blk_max)
        return m, alpha, accs, scores(k + 1)

    trips = jnp.maximum(n_tiles - 2, 0)

    def drain(state):
        m, alpha, accs, blk_max = state
        accs = accumulate(visit(jnp.maximum(n_tiles - 2, 0)), alpha, accs)
        m, alpha = probs(m, s_ref[...], blk_max, valid=n_tiles >= 2)
        acc0, acc1 = accumulate(visit(n_tiles - 1), alpha, accs)
        o0 = acc0[0:HEAD_DIM] / acc0[HEAD_DIM:HEAD_DIM + 1]
        o1 = acc1[0:HEAD_DIM] / acc1[HEAD_DIM:HEAD_DIM + 1]
        return jnp.concatenate([o0, o1], axis=0).T.astype(BF16)

    return state0, body, trips, drain


def _attention_kernel(n_seq, pt_ref, qt_ref, ext_ref, kb_ref, e_ref, va_ref,
                      qs_ref, qts_ref, kts_ref, vts_ref, lfs_ref, kt_hbm, vt_hbm, pp_hbm,
                      o_ref, os_ref, wq_refs, s_refs, p_refs, kbuf, vbuf, ppbuf, st_ref, sems):
    step = (pl.program_id(0) * pl.num_programs(1) + pl.program_id(1)) * pl.num_programs(2) + pl.program_id(2)
    copies = functools.partial(_page_copies, pt_ref, kt_hbm=kt_hbm, vt_hbm=vt_hbm, pp_hbm=pp_hbm,
                               kbuf=kbuf, vbuf=vbuf, ppbuf=ppbuf, sems=sems)

    @pl.when(step == 0)
    def _():
        os_ref[...] = jnp.zeros_like(os_ref)
        st_ref[...] = jnp.zeros_like(st_ref)
        for buf in range(2):
            for c in copies(buf, buf):
                c.start()

    pending = []
    for r in range(NQB // 2):
        seqs = [step * NQB + 2 * r + buf for buf in range(2)]
        for buf in range(2):
            for c in copies(seqs[buf], buf):
                c.wait()
        for buf in range(2):
            _decode_past(seqs[buf], buf, qs_ref, lfs_ref, kbuf, vbuf, ppbuf, os_ref, st_ref)
        for state, drain, rows in pending:
            o_ref[0, rows, :] = drain(state)
        blocks = []
        for buf in range(2):
            j = 2 * r + buf
            rows = slice(j * TQ, (j + 1) * TQ)
            scr = 2 * (r % 2) + buf
            blocks.append(_query_block(
                pl.program_id(2) * NQB + j, qt_ref[0, :, rows], ext_ref[0, :, :, rows],
                kb_ref, e_ref, va_ref, wq_refs.at[scr], s_refs.at[scr], p_refs.at[scr]) + (rows,))
        for buf in range(2):
            for c in copies(jnp.minimum(seqs[buf] + 2, n_seq - 2 + buf), buf):
                c.start()
        (state_a, body_a, trips_a, drain_a, rows_a), (state_b, body_b, trips_b, drain_b, rows_b) = blocks
        state_a, state_b = lax.fori_loop(
            0, trips_a, lambda k, st: (body_a(k, st[0]), body_b(k, st[1])), (state_a, state_b))
        state_b = lax.fori_loop(trips_a, trips_b, body_b, state_b)
        pending = [(state_a, drain_a, rows_a), (state_b, drain_b, rows_b)]
    for state, drain, rows in pending:
        o_ref[0, rows, :] = drain(state)

    @pl.when(step * NQB + NQB == n_seq)
    def _():
        for buf in range(2):
            for c in copies(n_seq - 2 + buf, buf):
                c.wait()
        _decode_finish(qts_ref, kts_ref, vts_ref, os_ref, st_ref)


def _attention(page_table, qt, ext, kb, e, va, qs, qts, kts, vts, lfs, kt_pages, vt_pages, pp):
    B, _, S = qt.shape
    nk = S // TK
    n_seq, n_pages = page_table.shape
    tq = NQB * TQ
    grid = (B, N_HEADS // 2, S // tq)
    assert NQB % 2 == 0 and n_seq == grid[0] * grid[1] * grid[2] * NQB, "one decode sequence per query block"
    resident = lambda a: pl.BlockSpec(a.shape, lambda b, p, g, pt: (0, 0))
    in_hbm = pl.BlockSpec(memory_space=pl.ANY)
    in_specs = [
        pl.BlockSpec((1, LANES, tq), lambda b, p, g, pt: (b, p, g)),
        pl.BlockSpec((1, 2, X_ROWS, tq), lambda b, p, g, pt: (b, p, 0, g)),
        pl.BlockSpec((1, S, LANES), lambda b, p, g, pt: (b, 0, p)),
        pl.BlockSpec((1, S, LANES), lambda b, p, g, pt: (b, 0, 0)),
        pl.BlockSpec((1, 2, nk, V_ROWS, TK), lambda b, p, g, pt: (b, p, 0, 0, 0)),
        resident(qs), resident(qts), resident(kts), resident(vts), resident(lfs),
        in_hbm, in_hbm, in_hbm]
    grid_spec = pltpu.PrefetchScalarGridSpec(
        num_scalar_prefetch=1, grid=grid, in_specs=in_specs,
        out_specs=(pl.BlockSpec((1, tq, LANES), lambda b, p, g, pt: (b, g, p)), resident(qts)),
        scratch_shapes=[pltpu.VMEM((4, 2 * LANES, 2 * TQ), BF16), pltpu.VMEM((4, KS, 2 * TQ), F32),
                        pltpu.VMEM((4, KS, 2 * TQ), BF16),
                        pltpu.VMEM((2, n_pages, D_ATT, PAGE), F32), pltpu.VMEM((2, n_pages, D_ATT, PAGE), F32),
                        pltpu.VMEM((2, n_pages, N_HEADS, 2 * PAGE), F32),
                        pltpu.VMEM((2 * N_HEADS, n_seq), F32), pltpu.SemaphoreType.DMA((2, 3))])
    return pl.pallas_call(
        functools.partial(_attention_kernel, n_seq),
        out_shape=(jax.ShapeDtypeStruct((B, S, D_ATT), BF16), jax.ShapeDtypeStruct((D_ATT, n_seq), F32)),
        grid_spec=grid_spec,
        compiler_params=pltpu.CompilerParams(
            dimension_semantics=("arbitrary", "arbitrary", "arbitrary"),
            vmem_limit_bytes=VMEM_LIMIT_BYTES),
        name="attention",
    )(page_table, qt, ext, kb, e, va, qs, qts, kts, vts, lfs, kt_pages, vt_pages, pp)


def _merge_kernel(x_ref, a_ref, ga_ref, m_ref, wo_ref, y_ref):
    a = (a_ref[0].astype(F32) * ga_ref[0].astype(F32)).astype(BF16)
    y_ref[0] = (x_ref[0] + _dot(a, wo_ref[0:D_ATT, :]) + _dot(m_ref[0], wo_ref[D_ATT:D_ATT + D_CHK, :]))


def _merge(x, a, ga, m, wo, tm):
    B, S, _ = x.shape
    row_blk = lambda width: pl.BlockSpec((1, tm, width), lambda b, s: (b, s, 0))
    return pl.pallas_call(
        _merge_kernel,
        out_shape=jax.ShapeDtypeStruct((B, S, D_MODEL), F32),
        grid=(B, S // tm),
        in_specs=[row_blk(D_MODEL), row_blk(D_ATT), row_blk(D_ATT), row_blk(D_CHK),
                  pl.BlockSpec((D_ATT + D_CHK, D_MODEL), lambda b, s: (0, 0))],
        out_specs=row_blk(D_MODEL),
        compiler_params=pltpu.CompilerParams(
            dimension_semantics=("arbitrary", "arbitrary"), vmem_limit_bytes=VMEM_LIMIT_BYTES),
        name="merge",
    )(x, a, ga, m, wo)


def _rms_heads_t(xt, gain_col):
    n = xt.shape[-1]
    x3 = xt.reshape(N_HEADS, HEAD_DIM, n)
    ssq = jnp.sum(x3 * x3, axis=1, keepdims=True)
    return (x3 * lax.rsqrt(ssq * (1.0 / HEAD_DIM) + EPS)).reshape(D_ATT, n) * gain_col


def _sample_proj_kernel(x_ref, gn_ref, w_ref, wt_ref, bft_ref, gqt_ref, gkt_ref, gk_ref, gv_ref,
                        w00_ref, b0_ref,
                        k_out, gvn_out, qt_out, kt_out, vt_out, lft_out, ga_out, m_out):
    h = _rms_rows(x_ref[...], gn_ref[...]).astype(BF16)
    t = _dot_nt(wt_ref[...], h)
    qt_out[...] = _rms_heads_t(t[0:D_ATT], gqt_ref[...]) * SCALE
    kt_out[...] = _rms_heads_t(t[D_ATT:2 * D_ATT], gkt_ref[...])
    vt_out[...] = t[2 * D_ATT:3 * D_ATT]
    lft_out[...] = _log_sigmoid(t[3 * D_ATT:3 * D_ATT + N_HEADS] + bft_ref[...])
    k_out[...] = _rms_head_pairs(_dot(h, w_ref[:, W_K:W_K + D_ATT]), gk_ref[...])
    ga_out[...] = _silu(_dot(h, w_ref[:, W_ZA:W_ZA + D_ATT])).astype(BF16)
    u = _dot(h, w_ref[:, W_U:W_U + D_CHK])
    gvn = _rms_groups(_dot(h, w_ref[:, W_GV:W_GV + D_CHK]), gv_ref[...])
    gvn_out[...] = gvn
    gate_c = _silu(_dot(h, w_ref[:, W_ZC:W_ZC + D_CHK]))
    m_out[...] = (u * (w00_ref[...] * gvn + b0_ref[...]) * gate_c).astype(BF16)


def _sample_proj(x, gn, w, wt_s, bft, gqt, gkt, gk, gv, w00, b0):
    n = x.shape[0]
    full = lambda a: pl.BlockSpec(a.shape, lambda i: (0,) * a.ndim)
    args = (x, gn, w, wt_s, bft, gqt, gkt, gk, gv, w00, b0)
    out_shape = (
        jax.ShapeDtypeStruct((n, D_ATT), F32), jax.ShapeDtypeStruct((n, D_CHK), F32),
        jax.ShapeDtypeStruct((D_ATT, n), F32), jax.ShapeDtypeStruct((D_ATT, n), F32),
        jax.ShapeDtypeStruct((D_ATT, n), F32), jax.ShapeDtypeStruct((N_HEADS, n), F32),
        jax.ShapeDtypeStruct((n, D_ATT), BF16), jax.ShapeDtypeStruct((n, D_CHK), BF16),
    )
    return pl.pallas_call(
        _sample_proj_kernel,
        out_shape=out_shape,
        grid=(1,),
        in_specs=[full(a) for a in args],
        out_specs=tuple(pl.BlockSpec(o.shape, lambda i: (0, 0)) for o in out_shape),
        compiler_params=pltpu.CompilerParams(
            dimension_semantics=("arbitrary",), vmem_limit_bytes=VMEM_LIMIT_BYTES),
        name="sample_proj",
    )(*args)


def _page_suffix_matrix():
    pos = np.arange(PAGE)
    later = pos[:, None] > pos[None, :]
    return np.concatenate([later, np.ones((PAGE, PAGE), bool)], axis=1).astype(np.float32)


def _page_suffix_kernel(lf_ref, mat_ref, o_ref):
    a1, a2, a3 = _split3(lf_ref[...])
    mat = mat_ref[...]
    o_ref[...] = _dot(a1, mat) + _dot(a2, mat) + _dot(a3, mat)


def _page_suffix(lf_rows, mat):
    n = lf_rows.shape[0]
    return pl.pallas_call(
        _page_suffix_kernel,
        out_shape=jax.ShapeDtypeStruct((n, 2 * PAGE), F32),
        grid=(n // PP_ROWS,),
        in_specs=[pl.BlockSpec((PP_ROWS, PAGE), lambda i: (i, 0)),
                  pl.BlockSpec((PAGE, 2 * PAGE), lambda i: (0, 0))],
        out_specs=pl.BlockSpec((PP_ROWS, 2 * PAGE), lambda i: (i, 0)),
        compiler_params=pltpu.CompilerParams(
            dimension_semantics=("arbitrary",), vmem_limit_bytes=VMEM_LIMIT_BYTES),
        name="page_suffix",
    )(lf_rows, mat)


def _layer(xp, xs, cache_k, cache_v, cache_logf, page_table, g_norm, w_in, b_f, g_q, g_k, g_v,
           w_s, b_s, w_out):
    B, S, _ = xp.shape
    nb = xs.shape[0]
    n_phys = cache_k.shape[0]

    cols = lambda a, n: w_in[:, a:a + n]
    wf = cols(_F0, N_HEADS)
    w = jnp.concatenate([cols(_K0, D_ATT), cols(_ZA0, D_ATT), cols(_U0, D_CHK), cols(_GV0, D_CHK),
                         cols(_ZC0, D_CHK)], axis=1).astype(BF16)
    pad8 = jnp.zeros((D_MODEL, 8), F32)
    wt = jnp.concatenate([cols(_Q0, D_ATT), cols(_V0, D_ATT), wf, pad8], axis=1).T.astype(BF16)
    wt_s = jnp.concatenate([cols(_Q0, D_ATT), cols(_K0, D_ATT), cols(_V0, D_ATT), wf, pad8],
                           axis=1).T.astype(BF16)
    wo = w_out.astype(BF16)
    gn = g_norm.reshape(1, D_MODEL)
    bft = b_f.reshape(N_HEADS, 1)
    gq = jnp.tile(g_q, N_HEADS).reshape(1, D_ATT)
    gqt = gq.reshape(D_ATT, 1)
    gk = jnp.tile(g_k, N_HEADS).reshape(1, D_ATT)
    gkt = gk.reshape(D_ATT, 1)
    gv = g_v.reshape(1, D_CHK)
    bsb = jnp.broadcast_to(b_s[:, :, None], (N_GROUPS, CHUNK, CH_GROUP))
    w00 = jnp.repeat(w_s[:, 0, 0], CH_GROUP).reshape(1, D_CHK)
    b0 = jnp.repeat(b_s[:, 0], CH_GROUP).reshape(1, D_CHK)
    utri = jnp.asarray(np.triu(np.ones((TM, TM), np.float32)), BF16)

    k_p, vt_p, lft_p, kb, e, qt, ext, va, ga, m = _prompt_proj(
        xp, gn, w, wt, bft, gqt, gk, gv, w_s, bsb, utri)
    xs2 = xs.reshape(nb, D_MODEL)
    k_s, gvn_s, qts, kts, vts, lfs, ga_s, m_s = _sample_proj(
        xs2, gn, w, wt_s, bft, gqt, gkt, gk, gv, w00, b0)

    lf_rows = jnp.transpose(cache_logf, (0, 2, 1)).reshape(n_phys * N_HEADS, PAGE)
    kt_pages = jnp.transpose(cache_k, (0, 2, 3, 1)).reshape(n_phys, D_ATT, PAGE)
    vt_pages = jnp.transpose(cache_v, (0, 2, 3, 1)).reshape(n_phys, D_ATT, PAGE)
    pp = _page_suffix(lf_rows, jnp.asarray(_page_suffix_matrix(), BF16))
    pp = pp.reshape(n_phys, N_HEADS, 2 * PAGE)
    attn, attn_s_t = _attention(page_table, qt, ext, kb, e, va, qts.T, qts, kts, vts, lfs,
                                kt_pages, vt_pages, pp)

    y_p = _merge(xp, attn, ga, m, wo, 2 * TM)
    y_s = _merge(xs2.reshape(1, nb, D_MODEL), attn_s_t.T.reshape(1, nb, D_ATT).astype(BF16),
                 ga_s.reshape(1, nb, D_ATT), m_s.reshape(1, nb, D_CHK), wo, nb)

    v_p = jnp.transpose(vt_p.reshape(B, N_HEADS, HEAD_DIM, S), (0, 3, 1, 2))
    lf_p = jnp.transpose(lft_p, (0, 2, 1))
    return (y_p, y_s.reshape(nb, 1, D_MODEL),
            k_p.reshape(B, S, N_HEADS, HEAD_DIM), v_p, lf_p,
            k_s.reshape(nb, 1, N_HEADS, HEAD_DIM), vts.T.reshape(nb, 1, N_HEADS, HEAD_DIM),
            lfs.T.reshape(nb, 1, N_HEADS), gvn_s.reshape(nb, 1, N_GROUPS, CH_GROUP))


def kernel(x_prompt, x_sample, cache_k, cache_v, cache_logf, page_table, g_norm, w_in, b_f, g_q, g_k,
           g_v, w_s, b_s, w_out):
    depth = w_in.shape[0]
    assert x_sample.shape[1] == 1, "one new token per decode sequence"
    xp, xs = x_prompt, x_sample
    outs = []
    for l in range(depth):
        res = _layer(xp, xs, cache_k[l], cache_v[l], cache_logf[l], page_table, g_norm[l], w_in[l],
                     b_f[l], g_q[l], g_k[l], g_v[l], w_s[l], b_s[l], w_out[l])
        xp, xs = res[0], res[1]
        outs.append(res[2:])
    stacked = tuple(jnp.stack([o[i] for o in outs]) for i in range(7))
    return (xp, xs) + stacked
```

```python
import functools

import numpy as np
import jax
import jax.numpy as jnp
from jax import lax
from jax.experimental import pallas as pl
from jax.experimental.pallas import tpu as pltpu

F32 = jnp.float32
BF16 = jnp.bfloat16

D_MODEL = 1024
N_HEADS = 8
HEAD_DIM = 64
D_ATT = N_HEADS * HEAD_DIM
N_GROUPS = 4
CH_GROUP = 128
D_CHK = N_GROUPS * CH_GROUP
CHUNK = 128
PAGE = 128
EPS = 1e-6
SCALE = HEAD_DIM ** -0.5
LOG2E = 1.4426950408889634

LANES = 128
VMEM_LIMIT_BYTES = 56 * 1024 * 1024

TM = 512
TQ = 512
TK = 256
KS = TQ
NQB = 4
V_ROWS = 80
X_ROWS = 32
N_SPLIT = 3
SEL_ROWS = N_SPLIT * N_HEADS
PP_ROWS = 2048

_Q0, _K0, _V0, _F0 = 0, D_ATT, 2 * D_ATT, 3 * D_ATT
_ZA0 = 3 * D_ATT + N_HEADS
_U0 = _ZA0 + D_ATT
_GV0 = _U0 + D_CHK
_ZC0 = _GV0 + D_CHK
W_K, W_ZA, W_U, W_GV, W_ZC = (i * 512 for i in range(5))
W_COLS = 5 * 512
WT_ROWS = 2 * D_ATT + 16


def _log_sigmoid(x):
    return jnp.minimum(x, 0.0) - jnp.log1p(jnp.exp(-jnp.abs(x)))


def _silu(x):
    return x / (1.0 + jnp.exp(-x))


def _rms_rows(x, gain):
    ms = jnp.sum(x * x, axis=-1, keepdims=True) * (1.0 / x.shape[-1])
    return x * lax.rsqrt(ms + EPS) * gain


def _rms_head_pairs(x, gain):
    outs = []
    for p in range(x.shape[-1] // LANES):
        xp = x[:, p * LANES:(p + 1) * LANES]
        sq = xp * xp
        lo = lax.broadcasted_iota(jnp.int32, xp.shape, 1) < HEAD_DIM
        s_lo = jnp.sum(jnp.where(lo, sq, 0.0), axis=-1, keepdims=True)
        s_hi = jnp.sum(jnp.where(lo, 0.0, sq), axis=-1, keepdims=True)
        r = jnp.where(lo, lax.rsqrt(s_lo * (1.0 / HEAD_DIM) + EPS),
                      lax.rsqrt(s_hi * (1.0 / HEAD_DIM) + EPS))
        outs.append(xp * r * gain[:, p * LANES:(p + 1) * LANES])
    return jnp.concatenate(outs, axis=-1)


def _rms_groups(x, gain):
    outs = []
    for g in range(N_GROUPS):
        xg = x[:, g * CH_GROUP:(g + 1) * CH_GROUP]
        outs.append(_rms_rows(xg, gain[:, g * CH_GROUP:(g + 1) * CH_GROUP]))
    return jnp.concatenate(outs, axis=-1)


def _split3(a):
    a1 = a.astype(BF16)
    r1 = a - a1.astype(F32)
    a2 = r1.astype(BF16)
    a3 = (r1 - a2.astype(F32)).astype(BF16)
    return a1, a2, a3


def _dot(a, b):
    return jnp.dot(a, b, preferred_element_type=F32)


def _dot_nt(a, b):
    return lax.dot_general(a, b, (((1,), (1,)), ((), ())), preferred_element_type=F32)


def _prompt_proj_kernel(x_ref, gn_ref, w_ref, wt_ref, bft_ref, gqt_ref, gk_ref, gv_ref,
                        ws_ref, bsb_ref, utri_ref,
                        k_out, vt_out, lft_out, kb_out, e_out, qt_out, ext_out, va_out, ga_out, m_out,
                        carry_t):
    s_idx = pl.program_id(1)

    @pl.when(s_idx == 0)
    def _():
        carry_t[...] = jnp.zeros_like(carry_t)

    x = x_ref[0]
    h = _rms_rows(x, gn_ref[...]).astype(BF16)

    k = _rms_head_pairs(_dot(h, w_ref[:, W_K:W_K + D_ATT]), gk_ref[...])
    k_out[0] = k
    kb_out[0] = k.astype(BF16)

    t = _dot_nt(wt_ref[...], h)
    qt_out[0] = (_rms_heads_t(t[0:D_ATT], gqt_ref[...]) * (SCALE * LOG2E)).astype(BF16)

    vt = t[D_ATT:2 * D_ATT]
    vt_out[0] = vt
    row16 = lax.broadcasted_iota(jnp.int32, (V_ROWS - HEAD_DIM, TK), 0)
    tail = jnp.where(row16 == 0, 1.0, 0.0)
    for hh in range(N_HEADS):
        for kb in range(TM // TK):
            blk = vt[hh * HEAD_DIM:(hh + 1) * HEAD_DIM, kb * TK:(kb + 1) * TK]
            va_out[0, hh, kb] = jnp.concatenate([blk, tail], axis=0).astype(BF16)

    lft = _log_sigmoid(t[2 * D_ATT:2 * D_ATT + N_HEADS] + bft_ref[...])
    lft_out[0] = lft
    t1, t2, t3 = _split3(lft)
    utri = utri_ref[...]
    ct = carry_t[:, 0:1] + (_dot(t1, utri) + _dot(t2, utri) + _dot(t3, utri))
    carry_t[...] = jnp.broadcast_to(ct[:, TM - 1:TM], carry_t.shape)
    ct1, ct2, ct3 = (piece.astype(F32) for piece in _split3(ct * LOG2E))
    rid = lax.broadcasted_iota(jnp.int32, (X_ROWS, TM), 0)
    for hh in range(N_HEADS):
        sel = jnp.where(rid < SEL_ROWS, jnp.where((rid & (N_HEADS - 1)) == hh, 1.0, 0.0), 0.0)
        ext = jnp.where(rid == SEL_ROWS, ct1[hh:hh + 1],
                        jnp.where(rid == SEL_ROWS + 1, ct2[hh:hh + 1],
                                  jnp.where(rid == SEL_ROWS + 2, ct3[hh:hh + 1], sel)))
        ext_out[0, hh] = ext.astype(BF16)
    rid8 = lax.broadcasted_iota(jnp.int32, (N_HEADS, TM), 0)
    ones = jnp.where(rid8 < N_SPLIT, 1.0, 0.0)
    e_t = jnp.concatenate([-ct1, -ct2, -ct3, ones,
                           jnp.zeros((LANES - SEL_ROWS - N_HEADS, TM), F32)], axis=0)
    e_out[0] = e_t.T.astype(BF16)

    ga_out[0] = _silu(_dot(h, w_ref[:, W_ZA:W_ZA + D_ATT])).astype(BF16)
    u = _dot(h, w_ref[:, W_U:W_U + D_CHK])
    gvn = _rms_groups(_dot(h, w_ref[:, W_GV:W_GV + D_CHK]), gv_ref[...]).astype(BF16)
    gate_c = _silu(_dot(h, w_ref[:, W_ZC:W_ZC + D_CHK]))
    tri = (lax.broadcasted_iota(jnp.int32, (CHUNK, CHUNK), 1)
           <= lax.broadcasted_iota(jnp.int32, (CHUNK, CHUNK), 0))
    for g in range(N_GROUPS):
        wg = jnp.where(tri, ws_ref[g], 0.0).astype(BF16)
        cs = slice(g * CH_GROUP, (g + 1) * CH_GROUP)
        for cc in range(TM // CHUNK):
            rs = slice(cc * CHUNK, (cc + 1) * CHUNK)
            mix = _dot(wg, gvn[rs, cs]) + bsb_ref[g]
            m_out[0, rs, cs] = (u[rs, cs] * mix * gate_c[rs, cs]).astype(BF16)


def _prompt_proj(x, gn, w, wt, bft, gqt, gk, gv, ws, bsb, utri):
    B, S, _ = x.shape
    nk = S // TK
    const2 = lambda shape: pl.BlockSpec(shape, lambda b, s: (0, 0))
    const3 = lambda shape: pl.BlockSpec(shape, lambda b, s: (0, 0, 0))
    row_blk = lambda width: pl.BlockSpec((1, TM, width), lambda b, s: (b, s, 0))
    col_blk = lambda rows: pl.BlockSpec((1, rows, TM), lambda b, s: (b, 0, s))
    out_shape = (
        jax.ShapeDtypeStruct((B, S, D_ATT), F32),
        jax.ShapeDtypeStruct((B, D_ATT, S), F32),
        jax.ShapeDtypeStruct((B, N_HEADS, S), F32),
        jax.ShapeDtypeStruct((B, S, D_ATT), BF16),
        jax.ShapeDtypeStruct((B, S, LANES), BF16),
        jax.ShapeDtypeStruct((B, D_ATT, S), BF16),
        jax.ShapeDtypeStruct((B, N_HEADS, X_ROWS, S), BF16),
        jax.ShapeDtypeStruct((B, N_HEADS, nk, V_ROWS, TK), BF16),
        jax.ShapeDtypeStruct((B, S, D_ATT), BF16),
        jax.ShapeDtypeStruct((B, S, D_CHK), BF16),
    )
    out_specs = (
        row_blk(D_ATT), col_blk(D_ATT), col_blk(N_HEADS), row_blk(D_ATT), row_blk(LANES),
        col_blk(D_ATT),
        pl.BlockSpec((1, N_HEADS, X_ROWS, TM), lambda b, s: (b, 0, 0, s)),
        pl.BlockSpec((1, N_HEADS, TM // TK, V_ROWS, TK), lambda b, s: (b, 0, s, 0, 0)),
        row_blk(D_ATT), row_blk(D_CHK),
    )
    in_specs = [
        row_blk(D_MODEL),
        const2((1, D_MODEL)),
        const2((D_MODEL, W_COLS)),
        const2((WT_ROWS, D_MODEL)),
        const2((N_HEADS, 1)),
        const2((D_ATT, 1)),
        const2((1, D_ATT)),
        const2((1, D_CHK)),
        const3((N_GROUPS, CHUNK, CHUNK)),
        const3((N_GROUPS, CHUNK, CH_GROUP)),
        const2((TM, TM)),
    ]
    return pl.pallas_call(
        _prompt_proj_kernel,
        out_shape=out_shape,
        grid=(B, S // TM),
        in_specs=in_specs,
        out_specs=out_specs,
        scratch_shapes=[pltpu.VMEM((N_HEADS, LANES), F32)],
        compiler_params=pltpu.CompilerParams(
            dimension_semantics=("arbitrary", "arbitrary"), vmem_limit_bytes=VMEM_LIMIT_BYTES),
        name="prompt_proj",
    )(x, gn, w, wt, bft, gqt, gk, gv, ws, bsb, utri)


def _head_sum(x):
    return jnp.sum(x.reshape(N_HEADS, HEAD_DIM, x.shape[-1]), axis=1)


def _per_dim(x):
    return jnp.broadcast_to(x[:, None, :], (N_HEADS, HEAD_DIM, x.shape[-1])).reshape(D_ATT, x.shape[-1])


def _page_copies(pt_ref, seq, slot, kt_hbm, vt_hbm, pp_hbm, kbuf, vbuf, ppbuf, sems):
    copies = []
    for j in range(kbuf.shape[1]):
        page = pt_ref[seq, j]
        copies.append(pltpu.make_async_copy(kt_hbm.at[page], kbuf.at[slot, j], sems.at[slot, 0]))
        copies.append(pltpu.make_async_copy(vt_hbm.at[page], vbuf.at[slot, j], sems.at[slot, 1]))
        copies.append(pltpu.make_async_copy(pp_hbm.at[page], ppbuf.at[slot, j], sems.at[slot, 2]))
    return copies


def _decode_past(seq, slot, q_ref, lfs_ref, kbuf, vbuf, ppbuf, os_ref, st_ref):
    n_pages = kbuf.shape[1]
    n_seq = os_ref.shape[-1]
    lane_is_seq = lambda rows: lax.broadcasted_iota(jnp.int32, (rows, n_seq), 1) == seq
    own = (lax.broadcasted_iota(jnp.int32, (N_HEADS, D_ATT), 0)
           == lax.broadcasted_iota(jnp.int32, (N_HEADS, D_ATT), 1) // HEAD_DIM)
    qbd = jnp.where(own, jnp.broadcast_to(q_ref[pl.ds(seq, 1), :], (N_HEADS, D_ATT)), 0.0)
    lf_new = jnp.sum(jnp.where(lane_is_seq(N_HEADS), lfs_ref[...], 0.0), axis=1, keepdims=True)

    off = jnp.broadcast_to(lf_new, (N_HEADS, PAGE))
    scores = [None] * n_pages
    for p in range(n_pages - 1, -1, -1):
        scores[p] = _dot(qbd, kbuf[slot, p]) + ppbuf[slot, p, :, 0:PAGE] + off
        off = off + ppbuf[slot, p, :, PAGE:2 * PAGE]
    top = scores[0]
    for p in range(1, n_pages):
        top = jnp.maximum(top, scores[p])
    m = jnp.max(top, axis=-1, keepdims=True)
    probs = [jnp.exp(s - m) for s in scores]
    p_sum = probs[0]
    for p in range(1, n_pages):
        p_sum = p_sum + probs[p]
    cols = []
    for hh in range(N_HEADS):
        rows = slice(hh * HEAD_DIM, (hh + 1) * HEAD_DIM)
        acc = vbuf[slot, 0, rows, :] * probs[0][hh:hh + 1, :]
        for p in range(1, n_pages):
            acc = acc + vbuf[slot, p, rows, :] * probs[p][hh:hh + 1, :]
        cols.append(jnp.sum(acc, axis=-1, keepdims=True))
    os_ref[...] = jnp.where(lane_is_seq(D_ATT), jnp.concatenate(cols, axis=0), os_ref[...])
    stats = jnp.concatenate([m, jnp.sum(p_sum, axis=-1, keepdims=True)], axis=0)
    st_ref[...] = jnp.where(lane_is_seq(2 * N_HEADS), stats, st_ref[...])


def _decode_finish(qts_ref, kts_ref, vts_ref, os_ref, st_ref):
    s_self = _head_sum(qts_ref[...] * kts_ref[...])
    m_past, l_past = st_ref[0:N_HEADS, :], st_ref[N_HEADS:2 * N_HEADS, :]
    m = jnp.maximum(m_past, s_self)
    w_past, p_self = jnp.exp(m_past - m), jnp.exp(s_self - m)
    os_ref[...] = ((os_ref[...] * _per_dim(w_past) + _per_dim(p_self) * vts_ref[...])
                   / _per_dim(l_past * w_past + p_self))


def _query_block(i, qt, ext, kb_ref, e_ref, va_ref, wq_ref, s_ref, p_ref):
    last_tile = (i * TQ) // KS
    n_tiles = last_tile + 1

    wq_ref[...] = jnp.zeros_like(wq_ref)
    wq_ref[0:HEAD_DIM, 0:TQ] = qt[0:HEAD_DIM, :]
    wq_ref[HEAD_DIM:2 * HEAD_DIM, TQ:2 * TQ] = qt[HEAD_DIM:2 * HEAD_DIM, :]
    wq_ref[LANES:LANES + X_ROWS, 0:TQ] = ext[0]
    wq_ref[LANES:LANES + X_ROWS, TQ:2 * TQ] = ext[1]

    visit = lambda k: jnp.where(k == 0, last_tile, k - 1)

    def scores(tile):
        off = pl.multiple_of(tile * KS, KS)
        ke = jnp.concatenate([kb_ref[0, pl.ds(off, KS), :], e_ref[0, pl.ds(off, KS), :]], axis=1)
        s = _dot(ke, wq_ref[...])
        s_ref[...] = s
        return jnp.max(s, axis=0, keepdims=True)

    def probs(m, s, blk_max, valid=None):
        m_new = jnp.maximum(m, blk_max)
        alpha = jnp.exp2(m - m_new)
        if valid is not None:
            alpha = jnp.where(valid, alpha, 1.0)
            p_ref[...] = jnp.exp2(s - jnp.where(valid, m_new, jnp.inf)).astype(BF16)
            return jnp.where(valid, m_new, m), alpha
        p_ref[...] = jnp.exp2(s - m_new).astype(BF16)
        return m_new, alpha

    def accumulate(tile, alpha, accs):
        out = []
        for hh in range(2):
            cols = slice(hh * TQ, (hh + 1) * TQ)
            acc = alpha[:, cols] * accs[hh]
            for kk in range(KS // TK):
                acc = acc + _dot(va_ref[0, hh, tile * (KS // TK) + kk], p_ref[kk * TK:(kk + 1) * TK, cols])
            out.append(acc)
        return tuple(out)

    scores(last_tile)
    krow = lax.broadcasted_iota(jnp.int32, (KS, 2 * TQ), 0) + (last_tile * KS - i * TQ)
    qcol = lax.broadcasted_iota(jnp.int32, (KS, 2 * TQ), 1) & (TQ - 1)
    s_diag = jnp.where(krow <= qcol, s_ref[...], -jnp.inf)
    zero = jnp.zeros((V_ROWS, TQ), F32)
    m, alpha = probs(jnp.full((1, 2 * TQ), -jnp.inf, F32), s_diag, jnp.max(s_diag, axis=0, keepdims=True))
    state0 = (m, alpha, (zero, zero), scores(0))

    def body(k, carry):
        m, alpha, accs, blk_max = carry
        accs = accumulate(visit(k), alpha, accs)
        m, alpha = probs(m, s_ref[...], blk_max)
        return m, alpha, accs, scores(k + 1)

    trips = jnp.maximum(n_tiles - 2, 0)

    def drain(state):
        m, alpha, accs, blk_max = state
        accs = accumulate(visit(jnp.maximum(n_tiles - 2, 0)), alpha, accs)
        m, alpha = probs(m, s_ref[...], blk_max, valid=n_tiles >= 2)
        acc0, acc1 = accumulate(visit(n_tiles - 1), alpha, accs)
        o0 = acc0[0:HEAD_DIM] / acc0[HEAD_DIM:HEAD_DIM + 1]
        o1 = acc1[0:HEAD_DIM] / acc1[HEAD_DIM:HEAD_DIM + 1]
        return jnp.concatenate([o0, o1], axis=0).T.astype(BF16)

    return state0, body, trips, drain


def _attention_kernel(n_seq, pt_ref, qt_ref, ext_ref, kb_ref, e_ref, va_ref,
                      qs_ref, qts_ref, kts_ref, vts_ref, lfs_ref, kt_hbm, vt_hbm, pp_hbm,
                      o_ref, os_ref, wq_refs, s_refs, p_refs, kbuf, vbuf, ppbuf, st_ref, sems):
    step = (pl.program_id(0) * pl.num_programs(1) + pl.program_id(1)) * pl.num_programs(2) + pl.program_id(2)
    copies = functools.partial(_page_copies, pt_ref, kt_hbm=kt_hbm, vt_hbm=vt_hbm, pp_hbm=pp_hbm,
                               kbuf=kbuf, vbuf=vbuf, ppbuf=ppbuf, sems=sems)

    @pl.when(step == 0)
    def _():
        os_ref[...] = jnp.zeros_like(os_ref)
        st_ref[...] = jnp.zeros_like(st_ref)
        for buf in range(2):
            for c in copies(buf, buf):
                c.start()

    pending = []
    for r in range(NQB // 2):
        seqs = [step * NQB + 2 * r + buf for buf in range(2)]
        for buf in range(2):
            for c in copies(seqs[buf], buf):
                c.wait()
        for buf in range(2):
            _decode_past(seqs[buf], buf, qs_ref, lfs_ref, kbuf, vbuf, ppbuf, os_ref, st_ref)
        for buf in range(2):
            for c in copies(jnp.minimum(seqs[buf] + 2, n_seq - 2 + buf), buf):
                c.start()
        for state, drain, rows in pending:
            o_ref[0, rows, :] = drain(state)
        blocks = []
        for buf in range(2):
            j = 2 * r + buf
            rows = slice(j * TQ, (j + 1) * TQ)
            scr = 2 * (r % 2) + buf
            blocks.append(_query_block(
                pl.program_id(2) * NQB + j, qt_ref[0, :, rows], ext_ref[0, :, :, rows],
                kb_ref, e_ref, va_ref, wq_refs.at[scr], s_refs.at[scr], p_refs.at[scr]) + (rows,))
        (state_a, body_a, trips_a, drain_a, rows_a), (state_b, body_b, trips_b, drain_b, rows_b) = blocks
        both = lambda k, st: (body_a(k, st[0]), body_b(k, st[1]))
        pairs = trips_a // 2
        states = lax.fori_loop(0, pairs, lambda kk, st: both(2 * kk + 1, both(2 * kk, st)),
                               (state_a, state_b))
        state_a, state_b = lax.fori_loop(2 * pairs, trips_a, both, states)
        state_b = lax.fori_loop(trips_a, trips_b, body_b, state_b)
        pending = [(state_a, drain_a, rows_a), (state_b, drain_b, rows_b)]
    for state, drain, rows in pending:
        o_ref[0, rows, :] = drain(state)

    @pl.when(step * NQB + NQB == n_seq)
    def _():
        for buf in range(2):
            for c in copies(n_seq - 2 + buf, buf):
                c.wait()
        _decode_finish(qts_ref, kts_ref, vts_ref, os_ref, st_ref)


def _attention(page_table, qt, ext, kb, e, va, qs, qts, kts, vts, lfs, kt_pages, vt_pages, pp):
    B, _, S = qt.shape
    nk = S // TK
    n_seq, n_pages = page_table.shape
    tq = NQB * TQ
    grid = (B, N_HEADS // 2, S // tq)
    assert NQB % 2 == 0 and n_seq == grid[0] * grid[1] * grid[2] * NQB, "one decode sequence per query block"
    resident = lambda a: pl.BlockSpec(a.shape, lambda b, p, g, pt: (0, 0))
    in_hbm = pl.BlockSpec(memory_space=pl.ANY)
    in_specs = [
        pl.BlockSpec((1, LANES, tq), lambda b, p, g, pt: (b, p, g)),
        pl.BlockSpec((1, 2, X_ROWS, tq), lambda b, p, g, pt: (b, p, 0, g)),
        pl.BlockSpec((1, S, LANES), lambda b, p, g, pt: (b, 0, p)),
        pl.BlockSpec((1, S, LANES), lambda b, p, g, pt: (b, 0, 0)),
        pl.BlockSpec((1, 2, nk, V_ROWS, TK), lambda b, p, g, pt: (b, p, 0, 0, 0)),
        resident(qs), resident(qts), resident(kts), resident(vts), resident(lfs),
        in_hbm, in_hbm, in_hbm]
    grid_spec = pltpu.PrefetchScalarGridSpec(
        num_scalar_prefetch=1, grid=grid, in_specs=in_specs,
        out_specs=(pl.BlockSpec((1, tq, LANES), lambda b, p, g, pt: (b, g, p)), resident(qts)),
        scratch_shapes=[pltpu.VMEM((4, 2 * LANES, 2 * TQ), BF16), pltpu.VMEM((4, KS, 2 * TQ), F32),
                        pltpu.VMEM((4, KS, 2 * TQ), BF16),
                        pltpu.VMEM((2, n_pages, D_ATT, PAGE), F32), pltpu.VMEM((2, n_pages, D_ATT, PAGE), F32),
                        pltpu.VMEM((2, n_pages, N_HEADS, 2 * PAGE), F32),
                        pltpu.VMEM((2 * N_HEADS, n_seq), F32), pltpu.SemaphoreType.DMA((2, 3))])
    return pl.pallas_call(
        functools.partial(_attention_kernel, n_seq),
        out_shape=(jax.ShapeDtypeStruct((B, S, D_ATT), BF16), jax.ShapeDtypeStruct((D_ATT, n_seq), F32)),
        grid_spec=grid_spec,
        compiler_params=pltpu.CompilerParams(
            dimension_semantics=("arbitrary", "arbitrary", "arbitrary"),
            vmem_limit_bytes=VMEM_LIMIT_BYTES),
        name="attention",
    )(page_table, qt, ext, kb, e, va, qs, qts, kts, vts, lfs, kt_pages, vt_pages, pp)


def _merge_kernel(x_ref, a_ref, ga_ref, m_ref, wo_ref, y_ref):
    a = (a_ref[0].astype(F32) * ga_ref[0].astype(F32)).astype(BF16)
    y_ref[0] = (x_ref[0] + _dot(a, wo_ref[0:D_ATT, :]) + _dot(m_ref[0], wo_ref[D_ATT:D_ATT + D_CHK, :]))


def _merge(x, a, ga, m, wo, tm):
    B, S, _ = x.shape
    row_blk = lambda width: pl.BlockSpec((1, tm, width), lambda b, s: (b, s, 0))
    return pl.pallas_call(
        _merge_kernel,
        out_shape=jax.ShapeDtypeStruct((B, S, D_MODEL), F32),
        grid=(B, S // tm),
        in_specs=[row_blk(D_MODEL), row_blk(D_ATT), row_blk(D_ATT), row_blk(D_CHK),
                  pl.BlockSpec((D_ATT + D_CHK, D_MODEL), lambda b, s: (0, 0))],
        out_specs=row_blk(D_MODEL),
        compiler_params=pltpu.CompilerParams(
            dimension_semantics=("arbitrary", "arbitrary"), vmem_limit_bytes=VMEM_LIMIT_BYTES),
        name="merge",
    )(x, a, ga, m, wo)


def _rms_heads_t(xt, gain_col):
    n = xt.shape[-1]
    x3 = xt.reshape(N_HEADS, HEAD_DIM, n)
    ssq = jnp.sum(x3 * x3, axis=1, keepdims=True)
    return (x3 * lax.rsqrt(ssq * (1.0 / HEAD_DIM) + EPS)).reshape(D_ATT, n) * gain_col


def _sample_proj_kernel(x_ref, gn_ref, w_ref, wt_ref, bft_ref, gqt_ref, gkt_ref, gk_ref, gv_ref,
                        w00_ref, b0_ref,
                        k_out, gvn_out, qt_out, kt_out, vt_out, lft_out, ga_out, m_out):
    h = _rms_rows(x_ref[...], gn_ref[...]).astype(BF16)
    t = _dot_nt(wt_ref[...], h)
    qt_out[...] = _rms_heads_t(t[0:D_ATT], gqt_ref[...]) * SCALE
    kt_out[...] = _rms_heads_t(t[D_ATT:2 * D_ATT], gkt_ref[...])
    vt_out[...] = t[2 * D_ATT:3 * D_ATT]
    lft_out[...] = _log_sigmoid(t[3 * D_ATT:3 * D_ATT + N_HEADS] + bft_ref[...])
    k_out[...] = _rms_head_pairs(_dot(h, w_ref[:, W_K:W_K + D_ATT]), gk_ref[...])
    ga_out[...] = _silu(_dot(h, w_ref[:, W_ZA:W_ZA + D_ATT])).astype(BF16)
    u = _dot(h, w_ref[:, W_U:W_U + D_CHK])
    gvn = _rms_groups(_dot(h, w_ref[:, W_GV:W_GV + D_CHK]), gv_ref[...])
    gvn_out[...] = gvn
    gate_c = _silu(_dot(h, w_ref[:, W_ZC:W_ZC + D_CHK]))
    m_out[...] = (u * (w00_ref[...] * gvn + b0_ref[...]) * gate_c).astype(BF16)


def _sample_proj(x, gn, w, wt_s, bft, gqt, gkt, gk, gv, w00, b0):
    n = x.shape[0]
    full = lambda a: pl.BlockSpec(a.shape, lambda i: (0,) * a.ndim)
    args = (x, gn, w, wt_s, bft, gqt, gkt, gk, gv, w00, b0)
    out_shape = (
        jax.ShapeDtypeStruct((n, D_ATT), F32), jax.ShapeDtypeStruct((n, D_CHK), F32),
        jax.ShapeDtypeStruct((D_ATT, n), F32), jax.ShapeDtypeStruct((D_ATT, n), F32),
        jax.ShapeDtypeStruct((D_ATT, n), F32), jax.ShapeDtypeStruct((N_HEADS, n), F32),
        jax.ShapeDtypeStruct((n, D_ATT), BF16), jax.ShapeDtypeStruct((n, D_CHK), BF16),
    )
    return pl.pallas_call(
        _sample_proj_kernel,
        out_shape=out_shape,
        grid=(1,),
        in_specs=[full(a) for a in args],
        out_specs=tuple(pl.BlockSpec(o.shape, lambda i: (0, 0)) for o in out_shape),
        compiler_params=pltpu.CompilerParams(
            dimension_semantics=("arbitrary",), vmem_limit_bytes=VMEM_LIMIT_BYTES),
        name="sample_proj",
    )(*args)


def _page_suffix_matrix():
    pos = np.arange(PAGE)
    later = pos[:, None] > pos[None, :]
    return np.concatenate([later, np.ones((PAGE, PAGE), bool)], axis=1).astype(np.float32)


def _page_suffix_kernel(lf_ref, mat_ref, o_ref):
    a1, a2, a3 = _split3(lf_ref[...])
    mat = mat_ref[...]
    o_ref[...] = _dot(a1, mat) + _dot(a2, mat) + _dot(a3, mat)


def _page_suffix(lf_rows, mat):
    n = lf_rows.shape[0]
    return pl.pallas_call(
        _page_suffix_kernel,
        out_shape=jax.ShapeDtypeStruct((n, 2 * PAGE), F32),
        grid=(n // PP_ROWS,),
        in_specs=[pl.BlockSpec((PP_ROWS, PAGE), lambda i: (i, 0)),
                  pl.BlockSpec((PAGE, 2 * PAGE), lambda i: (0, 0))],
        out_specs=pl.BlockSpec((PP_ROWS, 2 * PAGE), lambda i: (i, 0)),
        compiler_params=pltpu.CompilerParams(
            dimension_semantics=("arbitrary",), vmem_limit_bytes=VMEM_LIMIT_BYTES),
        name="page_suffix",
    )(lf_rows, mat)


def _layer(xp, xs, cache_k, cache_v, cache_logf, page_table, g_norm, w_in, b_f, g_q, g_k, g_v,
           w_s, b_s, w_out):
    B, S, _ = xp.shape
    nb = xs.shape[0]
    n_phys = cache_k.shape[0]

    cols = lambda a, n: w_in[:, a:a + n]
    wf = cols(_F0, N_HEADS)
    w = jnp.concatenate([cols(_K0, D_ATT), cols(_ZA0, D_ATT), cols(_U0, D_CHK), cols(_GV0, D_CHK),
                         cols(_ZC0, D_CHK)], axis=1).astype(BF16)
    pad8 = jnp.zeros((D_MODEL, 8), F32)
    wt = jnp.concatenate([cols(_Q0, D_ATT), cols(_V0, D_ATT), wf, pad8], axis=1).T.astype(BF16)
    wt_s = jnp.concatenate([cols(_Q0, D_ATT), cols(_K0, D_ATT), cols(_V0, D_ATT), wf, pad8],
                           axis=1).T.astype(BF16)
    wo = w_out.astype(BF16)
    gn = g_norm.reshape(1, D_MODEL)
    bft = b_f.reshape(N_HEADS, 1)
    gq = jnp.tile(g_q, N_HEADS).reshape(1, D_ATT)
    gqt = gq.reshape(D_ATT, 1)
    gk = jnp.tile(g_k, N_HEADS).reshape(1, D_ATT)
    gkt = gk.reshape(D_ATT, 1)
    gv = g_v.reshape(1, D_CHK)
    bsb = jnp.broadcast_to(b_s[:, :, None], (N_GROUPS, CHUNK, CH_GROUP))
    w00 = jnp.repeat(w_s[:, 0, 0], CH_GROUP).reshape(1, D_CHK)
    b0 = jnp.repeat(b_s[:, 0], CH_GROUP).reshape(1, D_CHK)
    utri = jnp.asarray(np.triu(np.ones((TM, TM), np.float32)), BF16)

    k_p, vt_p, lft_p, kb, e, qt, ext, va, ga, m = _prompt_proj(
        xp, gn, w, wt, bft, gqt, gk, gv, w_s, bsb, utri)
    xs2 = xs.reshape(nb, D_MODEL)
    k_s, gvn_s, qts, kts, vts, lfs, ga_s, m_s = _sample_proj(
        xs2, gn, w, wt_s, bft, gqt, gkt, gk, gv, w00, b0)

    lf_rows = jnp.transpose(cache_logf, (0, 2, 1)).reshape(n_phys * N_HEADS, PAGE)
    kt_pages = jnp.transpose(cache_k, (0, 2, 3, 1)).reshape(n_phys, D_ATT, PAGE)
    vt_pages = jnp.transpose(cache_v, (0, 2, 3, 1)).reshape(n_phys, D_ATT, PAGE)
    pp = _page_suffix(lf_rows, jnp.asarray(_page_suffix_matrix(), BF16))
    pp = pp.reshape(n_phys, N_HEADS, 2 * PAGE)
    attn, attn_s_t = _attention(page_table, qt, ext, kb, e, va, qts.T, qts, kts, vts, lfs,
                                kt_pages, vt_pages, pp)

    y_p = _merge(xp, attn, ga, m, wo, 2 * TM)
    y_s = _merge(xs2.reshape(1, nb, D_MODEL), attn_s_t.T.reshape(1, nb, D_ATT).astype(BF16),
                 ga_s.reshape(1, nb, D_ATT), m_s.reshape(1, nb, D_CHK), wo, nb)

    v_p = jnp.transpose(vt_p.reshape(B, N_HEADS, HEAD_DIM, S), (0, 3, 1, 2))
    lf_p = jnp.transpose(lft_p, (0, 2, 1))
    return (y_p, y_s.reshape(nb, 1, D_MODEL),
            k_p.reshape(B, S, N_HEADS, HEAD_DIM), v_p, lf_p,
            k_s.reshape(nb, 1, N_HEADS, HEAD_DIM), vts.T.reshape(nb, 1, N_HEADS, HEAD_DIM),
            lfs.T.reshape(nb, 1, N_HEADS), gvn_s.reshape(nb, 1, N_GROUPS, CH_GROUP))


def kernel(x_prompt, x_sample, cache_k, cache_v, cache_logf, page_table, g_norm, w_in, b_f, g_q, g_k,
           g_v, w_s, b_s, w_out):
    depth = w_in.shape[0]
    assert x_sample.shape[1] == 1, "one new token per decode sequence"
    xp, xs = x_prompt, x_sample
    outs = []
    for l in range(depth):
        res = _layer(xp, xs, cache_k[l], cache_v[l], cache_logf[l], page_table, g_norm[l], w_in[l],
                     b_f[l], g_q[l], g_k[l], g_v[l], w_s[l], b_s[l], w_out[l])
        xp, xs = res[0], res[1]
        outs.append(res[2:])
    stacked = tuple(jnp.stack([o[i] for o in outs]) for i in range(7))
    return (xp, xs) + stacked
```

```python
import functools

import numpy as np
import jax
import jax.numpy as jnp
from jax import lax
from jax.experimental import pallas as pl
from jax.experimental.pallas import tpu as pltpu

F32 = jnp.float32
BF16 = jnp.bfloat16

D_MODEL = 1024
N_HEADS = 8
HEAD_DIM = 64
D_ATT = N_HEADS * HEAD_DIM
N_GROUPS = 4
CH_GROUP = 128
D_CHK = N_GROUPS * CH_GROUP
CHUNK = 128
PAGE = 128
EPS = 1e-6
SCALE = HEAD_DIM ** -0.5
LOG2E = 1.4426950408889634

LANES = 128
VMEM_LIMIT_BYTES = 56 * 1024 * 1024

TM = 512
TQ = 512
TK = 256
KS = TQ
NQB = 4
V_ROWS = 80
X_ROWS = 32
N_SPLIT = 3
SEL_ROWS = N_SPLIT * N_HEADS
PP_STEPS = 4

_Q0, _K0, _V0, _F0 = 0, D_ATT, 2 * D_ATT, 3 * D_ATT
_ZA0 = 3 * D_ATT + N_HEADS
_U0 = _ZA0 + D_ATT
_GV0 = _U0 + D_CHK
_ZC0 = _GV0 + D_CHK
W_K, W_ZA, W_U, W_GV, W_ZC = (i * 512 for i in range(5))
W_COLS = 5 * 512
WT_ROWS = 2 * D_ATT + 16


def _log_sigmoid(x):
    return jnp.minimum(x, 0.0) - jnp.log1p(jnp.exp(-jnp.abs(x)))


def _silu(x):
    return x / (1.0 + jnp.exp(-x))


def _rms_rows(x, gain):
    ms = jnp.sum(x * x, axis=-1, keepdims=True) * (1.0 / x.shape[-1])
    return x * lax.rsqrt(ms + EPS) * gain


def _rms_head_pairs(x, gain):
    outs = []
    for p in range(x.shape[-1] // LANES):
        xp = x[:, p * LANES:(p + 1) * LANES]
        sq = xp * xp
        lo = lax.broadcasted_iota(jnp.int32, xp.shape, 1) < HEAD_DIM
        s_lo = jnp.sum(jnp.where(lo, sq, 0.0), axis=-1, keepdims=True)
        s_hi = jnp.sum(jnp.where(lo, 0.0, sq), axis=-1, keepdims=True)
        r = jnp.where(lo, lax.rsqrt(s_lo * (1.0 / HEAD_DIM) + EPS),
                      lax.rsqrt(s_hi * (1.0 / HEAD_DIM) + EPS))
        outs.append(xp * r * gain[:, p * LANES:(p + 1) * LANES])
    return jnp.concatenate(outs, axis=-1)


def _rms_groups(x, gain):
    outs = []
    for g in range(N_GROUPS):
        xg = x[:, g * CH_GROUP:(g + 1) * CH_GROUP]
        outs.append(_rms_rows(xg, gain[:, g * CH_GROUP:(g + 1) * CH_GROUP]))
    return jnp.concatenate(outs, axis=-1)


def _split3(a):
    a1 = a.astype(BF16)
    r1 = a - a1.astype(F32)
    a2 = r1.astype(BF16)
    a3 = (r1 - a2.astype(F32)).astype(BF16)
    return a1, a2, a3


def _dot(a, b):
    return jnp.dot(a, b, preferred_element_type=F32)


def _dot_nt(a, b):
    return lax.dot_general(a, b, (((1,), (1,)), ((), ())), preferred_element_type=F32)


def _prompt_proj_kernel(x_ref, gn_ref, w_ref, wt_ref, bft_ref, gqt_ref, gk_ref, gv_ref,
                        ws_ref, bsb_ref, utri_ref,
                        k_out, vt_out, lft_out, kb_out, e_out, qt_out, ext_out, va_out, ga_out, m_out,
                        carry_t):
    s_idx = pl.program_id(1)

    @pl.when(s_idx == 0)
    def _():
        carry_t[...] = jnp.zeros_like(carry_t)

    x = x_ref[0]
    h = _rms_rows(x, gn_ref[...]).astype(BF16)

    k = _rms_head_pairs(_dot(h, w_ref[:, W_K:W_K + D_ATT]), gk_ref[...])
    k_out[0] = k
    kb_out[0] = k.astype(BF16)

    t = _dot_nt(wt_ref[...], h)
    qt_out[0] = (_rms_heads_t(t[0:D_ATT], gqt_ref[...]) * (SCALE * LOG2E)).astype(BF16)

    vt = t[D_ATT:2 * D_ATT]
    vt_out[0] = vt
    row16 = lax.broadcasted_iota(jnp.int32, (V_ROWS - HEAD_DIM, TK), 0)
    tail = jnp.where(row16 == 0, 1.0, 0.0)
    for hh in range(N_HEADS):
        for kb in range(TM // TK):
            blk = vt[hh * HEAD_DIM:(hh + 1) * HEAD_DIM, kb * TK:(kb + 1) * TK]
            va_out[0, hh, kb] = jnp.concatenate([blk, tail], axis=0).astype(BF16)

    lft = _log_sigmoid(t[2 * D_ATT:2 * D_ATT + N_HEADS] + bft_ref[...])
    lft_out[0] = lft
    t1, t2, t3 = _split3(lft)
    utri = utri_ref[...]
    ct = carry_t[:, 0:1] + (_dot(t1, utri) + _dot(t2, utri) + _dot(t3, utri))
    carry_t[...] = jnp.broadcast_to(ct[:, TM - 1:TM], carry_t.shape)
    ct1, ct2, ct3 = (piece.astype(F32) for piece in _split3(ct * LOG2E))
    rid = lax.broadcasted_iota(jnp.int32, (X_ROWS, TM), 0)
    for hh in range(N_HEADS):
        sel = jnp.where(rid < SEL_ROWS, jnp.where((rid & (N_HEADS - 1)) == hh, 1.0, 0.0), 0.0)
        ext = jnp.where(rid == SEL_ROWS, ct1[hh:hh + 1],
                        jnp.where(rid == SEL_ROWS + 1, ct2[hh:hh + 1],
                                  jnp.where(rid == SEL_ROWS + 2, ct3[hh:hh + 1], sel)))
        ext_out[0, hh] = ext.astype(BF16)
    rid8 = lax.broadcasted_iota(jnp.int32, (N_HEADS, TM), 0)
    ones = jnp.where(rid8 < N_SPLIT, 1.0, 0.0)
    e_t = jnp.concatenate([-ct1, -ct2, -ct3, ones,
                           jnp.zeros((LANES - SEL_ROWS - N_HEADS, TM), F32)], axis=0)
    e_out[0] = e_t.T.astype(BF16)

    ga_out[0] = _silu(_dot(h, w_ref[:, W_ZA:W_ZA + D_ATT])).astype(BF16)
    u = _dot(h, w_ref[:, W_U:W_U + D_CHK])
    gvn = _rms_groups(_dot(h, w_ref[:, W_GV:W_GV + D_CHK]), gv_ref[...]).astype(BF16)
    gate_c = _silu(_dot(h, w_ref[:, W_ZC:W_ZC + D_CHK]))
    tri = (lax.broadcasted_iota(jnp.int32, (CHUNK, CHUNK), 1)
           <= lax.broadcasted_iota(jnp.int32, (CHUNK, CHUNK), 0))
    for g in range(N_GROUPS):
        wg = jnp.where(tri, ws_ref[g], 0.0).astype(BF16)
        cs = slice(g * CH_GROUP, (g + 1) * CH_GROUP)
        for cc in range(TM // CHUNK):
            rs = slice(cc * CHUNK, (cc + 1) * CHUNK)
            mix = _dot(wg, gvn[rs, cs]) + bsb_ref[g]
            m_out[0, rs, cs] = (u[rs, cs] * mix * gate_c[rs, cs]).astype(BF16)


def _prompt_proj(x, gn, w, wt, bft, gqt, gk, gv, ws, bsb, utri):
    B, S, _ = x.shape
    nk = S // TK
    const2 = lambda shape: pl.BlockSpec(shape, lambda b, s: (0, 0))
    const3 = lambda shape: pl.BlockSpec(shape, lambda b, s: (0, 0, 0))
    row_blk = lambda width: pl.BlockSpec((1, TM, width), lambda b, s: (b, s, 0))
    col_blk = lambda rows: pl.BlockSpec((1, rows, TM), lambda b, s: (b, 0, s))
    out_shape = (
        jax.ShapeDtypeStruct((B, S, D_ATT), F32),
        jax.ShapeDtypeStruct((B, D_ATT, S), F32),
        jax.ShapeDtypeStruct((B, N_HEADS, S), F32),
        jax.ShapeDtypeStruct((B, S, D_ATT), BF16),
        jax.ShapeDtypeStruct((B, S, LANES), BF16),
        jax.ShapeDtypeStruct((B, D_ATT, S), BF16),
        jax.ShapeDtypeStruct((B, N_HEADS, X_ROWS, S), BF16),
        jax.ShapeDtypeStruct((B, N_HEADS, nk, V_ROWS, TK), BF16),
        jax.ShapeDtypeStruct((B, S, D_ATT), BF16),
        jax.ShapeDtypeStruct((B, S, D_CHK), BF16),
    )
    out_specs = (
        row_blk(D_ATT), col_blk(D_ATT), col_blk(N_HEADS), row_blk(D_ATT), row_blk(LANES),
        col_blk(D_ATT),
        pl.BlockSpec((1, N_HEADS, X_ROWS, TM), lambda b, s: (b, 0, 0, s)),
        pl.BlockSpec((1, N_HEADS, TM // TK, V_ROWS, TK), lambda b, s: (b, 0, s, 0, 0)),
        row_blk(D_ATT), row_blk(D_CHK),
    )
    in_specs = [
        row_blk(D_MODEL),
        const2((1, D_MODEL)),
        const2((D_MODEL, W_COLS)),
        const2((WT_ROWS, D_MODEL)),
        const2((N_HEADS, 1)),
        const2((D_ATT, 1)),
        const2((1, D_ATT)),
        const2((1, D_CHK)),
        const3((N_GROUPS, CHUNK, CHUNK)),
        const3((N_GROUPS, CHUNK, CH_GROUP)),
        const2((TM, TM)),
    ]
    return pl.pallas_call(
        _prompt_proj_kernel,
        out_shape=out_shape,
        grid=(B, S // TM),
        in_specs=in_specs,
        out_specs=out_specs,
        scratch_shapes=[pltpu.VMEM((N_HEADS, LANES), F32)],
        compiler_params=pltpu.CompilerParams(
            dimension_semantics=("arbitrary", "arbitrary"), vmem_limit_bytes=VMEM_LIMIT_BYTES),
        name="prompt_proj",
    )(x, gn, w, wt, bft, gqt, gk, gv, ws, bsb, utri)


def _head_sum(x):
    return jnp.sum(x.reshape(N_HEADS, HEAD_DIM, x.shape[-1]), axis=1)


def _per_dim(x):
    return jnp.broadcast_to(x[:, None, :], (N_HEADS, HEAD_DIM, x.shape[-1])).reshape(D_ATT, x.shape[-1])


def _page_copies(pt_ref, seq, slot, kt_hbm, vt_hbm, pp_hbm, kbuf, vbuf, ppbuf, sems):
    copies = []
    for j in range(kbuf.shape[1]):
        page = pt_ref[seq, j]
        copies.append(pltpu.make_async_copy(kt_hbm.at[page], kbuf.at[slot, j], sems.at[slot, 0]))
        copies.append(pltpu.make_async_copy(vt_hbm.at[page], vbuf.at[slot, j], sems.at[slot, 1]))
        copies.append(pltpu.make_async_copy(pp_hbm.at[page], ppbuf.at[slot, j], sems.at[slot, 2]))
    return copies


def _decode_past(seq, slot, q_ref, lfs_ref, kbuf, vbuf, ppbuf, os_ref, st_ref):
    n_pages = kbuf.shape[1]
    n_seq = os_ref.shape[-1]
    lane_is_seq = lambda rows: lax.broadcasted_iota(jnp.int32, (rows, n_seq), 1) == seq
    own = (lax.broadcasted_iota(jnp.int32, (N_HEADS, D_ATT), 0)
           == lax.broadcasted_iota(jnp.int32, (N_HEADS, D_ATT), 1) // HEAD_DIM)
    qbd = jnp.where(own, jnp.broadcast_to(q_ref[pl.ds(seq, 1), :], (N_HEADS, D_ATT)), 0.0)
    lf_new = jnp.sum(jnp.where(lane_is_seq(N_HEADS), lfs_ref[...], 0.0), axis=1, keepdims=True)

    off = jnp.broadcast_to(lf_new, (N_HEADS, PAGE))
    scores = [None] * n_pages
    for p in range(n_pages - 1, -1, -1):
        scores[p] = _dot(qbd, kbuf[slot, p]) + ppbuf[slot, p, :, 0:PAGE] + off
        off = off + ppbuf[slot, p, :, PAGE:2 * PAGE]
    top = scores[0]
    for p in range(1, n_pages):
        top = jnp.maximum(top, scores[p])
    m = jnp.max(top, axis=-1, keepdims=True)
    probs = [jnp.exp(s - m) for s in scores]
    p_sum = probs[0]
    for p in range(1, n_pages):
        p_sum = p_sum + probs[p]
    cols = []
    for hh in range(N_HEADS):
        rows = slice(hh * HEAD_DIM, (hh + 1) * HEAD_DIM)
        acc = vbuf[slot, 0, rows, :] * probs[0][hh:hh + 1, :]
        for p in range(1, n_pages):
            acc = acc + vbuf[slot, p, rows, :] * probs[p][hh:hh + 1, :]
        cols.append(jnp.sum(acc, axis=-1, keepdims=True))
    os_ref[...] = jnp.where(lane_is_seq(D_ATT), jnp.concatenate(cols, axis=0), os_ref[...])
    stats = jnp.concatenate([m, jnp.sum(p_sum, axis=-1, keepdims=True)], axis=0)
    st_ref[...] = jnp.where(lane_is_seq(2 * N_HEADS), stats, st_ref[...])


def _decode_finish(qts_ref, kts_ref, vts_ref, os_ref, st_ref):
    s_self = _head_sum(qts_ref[...] * kts_ref[...])
    m_past, l_past = st_ref[0:N_HEADS, :], st_ref[N_HEADS:2 * N_HEADS, :]
    m = jnp.maximum(m_past, s_self)
    w_past, p_self = jnp.exp(m_past - m), jnp.exp(s_self - m)
    os_ref[...] = ((os_ref[...] * _per_dim(w_past) + _per_dim(p_self) * vts_ref[...])
                   / _per_dim(l_past * w_past + p_self))


def _query_block(i, qt, ext, kb_ref, e_ref, va_ref, wq_ref, s_ref, p_ref):
    last_tile = (i * TQ) // KS
    n_tiles = last_tile + 1

    wq_ref[...] = jnp.zeros_like(wq_ref)
    wq_ref[0:HEAD_DIM, 0:TQ] = qt[0:HEAD_DIM, :]
    wq_ref[HEAD_DIM:2 * HEAD_DIM, TQ:2 * TQ] = qt[HEAD_DIM:2 * HEAD_DIM, :]
    wq_ref[LANES:LANES + X_ROWS, 0:TQ] = ext[0]
    wq_ref[LANES:LANES + X_ROWS, TQ:2 * TQ] = ext[1]

    visit = lambda k: jnp.where(k == 0, last_tile, k - 1)

    def scores(tile):
        off = pl.multiple_of(tile * KS, KS)
        ke = jnp.concatenate([kb_ref[0, pl.ds(off, KS), :], e_ref[0, pl.ds(off, KS), :]], axis=1)
        s = _dot(ke, wq_ref[...])
        s_ref[...] = s
        return jnp.max(s, axis=0, keepdims=True)

    def probs(m, s, blk_max, valid=None):
        m_new = jnp.maximum(m, blk_max)
        alpha = jnp.exp2(m - m_new)
        if valid is not None:
            alpha = jnp.where(valid, alpha, 1.0)
            p_ref[...] = jnp.exp2(s - jnp.where(valid, m_new, jnp.inf)).astype(BF16)
            return jnp.where(valid, m_new, m), alpha
        p_ref[...] = jnp.exp2(s - m_new).astype(BF16)
        return m_new, alpha

    def accumulate(tile, alpha, accs):
        out = []
        for hh in range(2):
            cols = slice(hh * TQ, (hh + 1) * TQ)
            acc = alpha[:, cols] * accs[hh]
            for kk in range(KS // TK):
                acc = acc + _dot(va_ref[0, hh, tile * (KS // TK) + kk], p_ref[kk * TK:(kk + 1) * TK, cols])
            out.append(acc)
        return tuple(out)

    scores(last_tile)
    krow = lax.broadcasted_iota(jnp.int32, (KS, 2 * TQ), 0) + (last_tile * KS - i * TQ)
    qcol = lax.broadcasted_iota(jnp.int32, (KS, 2 * TQ), 1) & (TQ - 1)
    s_diag = jnp.where(krow <= qcol, s_ref[...], -jnp.inf)
    zero = jnp.zeros((V_ROWS, TQ), F32)
    m, alpha = probs(jnp.full((1, 2 * TQ), -jnp.inf, F32), s_diag, jnp.max(s_diag, axis=0, keepdims=True))
    state0 = (m, alpha, (zero, zero), scores(0))

    def body(k, carry):
        m, alpha, accs, blk_max = carry
        accs = accumulate(visit(k), alpha, accs)
        m, alpha = probs(m, s_ref[...], blk_max)
        return m, alpha, accs, scores(k + 1)

    trips = jnp.maximum(n_tiles - 2, 0)

    def drain(state):
        m, alpha, accs, blk_max = state
        accs = accumulate(visit(jnp.maximum(n_tiles - 2, 0)), alpha, accs)
        m, alpha = probs(m, s_ref[...], blk_max, valid=n_tiles >= 2)
        acc0, acc1 = accumulate(visit(n_tiles - 1), alpha, accs)
        o0 = acc0[0:HEAD_DIM] / acc0[HEAD_DIM:HEAD_DIM + 1]
        o1 = acc1[0:HEAD_DIM] / acc1[HEAD_DIM:HEAD_DIM + 1]
        return jnp.concatenate([o0, o1], axis=0).T.astype(BF16)

    return state0, body, trips, drain


def _attention_kernel(n_seq, pt_ref, qt_ref, ext_ref, kb_ref, e_ref, va_ref,
                      qs_ref, qts_ref, kts_ref, vts_ref, lfs_ref, kt_hbm, vt_hbm, pp_hbm,
                      o_ref, os_ref, wq_refs, s_refs, p_refs, kbuf, vbuf, ppbuf, st_ref, sems):
    step = (pl.program_id(0) * pl.num_programs(1) + pl.program_id(1)) * pl.num_programs(2) + pl.program_id(2)
    copies = functools.partial(_page_copies, pt_ref, kt_hbm=kt_hbm, vt_hbm=vt_hbm, pp_hbm=pp_hbm,
                               kbuf=kbuf, vbuf=vbuf, ppbuf=ppbuf, sems=sems)

    @pl.when(step == 0)
    def _():
        os_ref[...] = jnp.zeros_like(os_ref)
        st_ref[...] = jnp.zeros_like(st_ref)
        for buf in range(2):
            for c in copies(buf, buf):
                c.start()

    pending = []
    for r in range(NQB // 2):
        seqs = [step * NQB + 2 * r + buf for buf in range(2)]
        for buf in range(2):
            for c in copies(seqs[buf], buf):
                c.wait()
        for buf in range(2):
            _decode_past(seqs[buf], buf, qs_ref, lfs_ref, kbuf, vbuf, ppbuf, os_ref, st_ref)
        for buf in range(2):
            for c in copies(jnp.minimum(seqs[buf] + 2, n_seq - 2 + buf), buf):
                c.start()
        for state, drain, rows in pending:
            o_ref[0, rows, :] = drain(state)
        blocks = []
        for buf in range(2):
            j = 2 * r + buf
            rows = slice(j * TQ, (j + 1) * TQ)
            scr = 2 * (r % 2) + buf
            blocks.append(_query_block(
                pl.program_id(2) * NQB + j, qt_ref[0, :, rows], ext_ref[0, :, :, rows],
                kb_ref, e_ref, va_ref, wq_refs.at[scr], s_refs.at[scr], p_refs.at[scr]) + (rows,))
        (state_a, body_a, trips_a, drain_a, rows_a), (state_b, body_b, trips_b, drain_b, rows_b) = blocks
        both = lambda k, st: (body_a(k, st[0]), body_b(k, st[1]))
        pairs = trips_a // 2
        states = lax.fori_loop(0, pairs, lambda kk, st: both(2 * kk + 1, both(2 * kk, st)),
                               (state_a, state_b))

        def tail(k, st):
            sa, sb = both(k, st)
            return sa, body_b(k + 1, sb)

        state_a, state_b = lax.fori_loop(2 * pairs, trips_a, tail, states)
        pending = [(state_a, drain_a, rows_a), (state_b, drain_b, rows_b)]
    for state, drain, rows in pending:
        o_ref[0, rows, :] = drain(state)

    @pl.when(step * NQB + NQB == n_seq)
    def _():
        for buf in range(2):
            for c in copies(n_seq - 2 + buf, buf):
                c.wait()
        _decode_finish(qts_ref, kts_ref, vts_ref, os_ref, st_ref)


def _attention(page_table, qt, ext, kb, e, va, qs, qts, kts, vts, lfs, kt_pages, vt_pages, pp):
    B, _, S = qt.shape
    nk = S // TK
    n_seq, n_pages = page_table.shape
    tq = NQB * TQ
    grid = (B, N_HEADS // 2, S // tq)
    assert NQB % 2 == 0 and n_seq == grid[0] * grid[1] * grid[2] * NQB, "one decode sequence per query block"
    resident = lambda a: pl.BlockSpec(a.shape, lambda b, p, g, pt: (0, 0))
    in_hbm = pl.BlockSpec(memory_space=pl.ANY)
    in_specs = [
        pl.BlockSpec((1, LANES, tq), lambda b, p, g, pt: (b, p, g)),
        pl.BlockSpec((1, 2, X_ROWS, tq), lambda b, p, g, pt: (b, p, 0, g)),
        pl.BlockSpec((1, S, LANES), lambda b, p, g, pt: (b, 0, p)),
        pl.BlockSpec((1, S, LANES), lambda b, p, g, pt: (b, 0, 0)),
        pl.BlockSpec((1, 2, nk, V_ROWS, TK), lambda b, p, g, pt: (b, p, 0, 0, 0)),
        resident(qs), resident(qts), resident(kts), resident(vts), resident(lfs),
        in_hbm, in_hbm, in_hbm]
    grid_spec = pltpu.PrefetchScalarGridSpec(
        num_scalar_prefetch=1, grid=grid, in_specs=in_specs,
        out_specs=(pl.BlockSpec((1, tq, LANES), lambda b, p, g, pt: (b, g, p)), resident(qts)),
        scratch_shapes=[pltpu.VMEM((4, 2 * LANES, 2 * TQ), BF16), pltpu.VMEM((4, KS, 2 * TQ), F32),
                        pltpu.VMEM((4, KS, 2 * TQ), BF16),
                        pltpu.VMEM((2, n_pages, D_ATT, PAGE), F32), pltpu.VMEM((2, n_pages, D_ATT, PAGE), F32),
                        pltpu.VMEM((2, n_pages, N_HEADS, 2 * PAGE), F32),
                        pltpu.VMEM((2 * N_HEADS, n_seq), F32), pltpu.SemaphoreType.DMA((2, 3))])
    return pl.pallas_call(
        functools.partial(_attention_kernel, n_seq),
        out_shape=(jax.ShapeDtypeStruct((B, S, D_ATT), BF16), jax.ShapeDtypeStruct((D_ATT, n_seq), F32)),
        grid_spec=grid_spec,
        compiler_params=pltpu.CompilerParams(
            dimension_semantics=("arbitrary", "arbitrary", "arbitrary"),
            vmem_limit_bytes=VMEM_LIMIT_BYTES),
        name="attention",
    )(page_table, qt, ext, kb, e, va, qs, qts, kts, vts, lfs, kt_pages, vt_pages, pp)


def _merge_kernel(x_ref, a_ref, ga_ref, m_ref, wo_ref, y_ref):
    a = (a_ref[0].astype(F32) * ga_ref[0].astype(F32)).astype(BF16)
    y_ref[0] = (x_ref[0] + _dot(a, wo_ref[0:D_ATT, :]) + _dot(m_ref[0], wo_ref[D_ATT:D_ATT + D_CHK, :]))


def _merge(x, a, ga, m, wo, tm):
    B, S, _ = x.shape
    row_blk = lambda width: pl.BlockSpec((1, tm, width), lambda b, s: (b, s, 0))
    return pl.pallas_call(
        _merge_kernel,
        out_shape=jax.ShapeDtypeStruct((B, S, D_MODEL), F32),
        grid=(B, S // tm),
        in_specs=[row_blk(D_MODEL), row_blk(D_ATT), row_blk(D_ATT), row_blk(D_CHK),
                  pl.BlockSpec((D_ATT + D_CHK, D_MODEL), lambda b, s: (0, 0))],
        out_specs=row_blk(D_MODEL),
        compiler_params=pltpu.CompilerParams(
            dimension_semantics=("arbitrary", "arbitrary"), vmem_limit_bytes=VMEM_LIMIT_BYTES),
        name="merge",
    )(x, a, ga, m, wo)


def _rms_heads_t(xt, gain_col):
    n = xt.shape[-1]
    x3 = xt.reshape(N_HEADS, HEAD_DIM, n)
    ssq = jnp.sum(x3 * x3, axis=1, keepdims=True)
    return (x3 * lax.rsqrt(ssq * (1.0 / HEAD_DIM) + EPS)).reshape(D_ATT, n) * gain_col


def _sample_proj_kernel(x_ref, gn_ref, w_ref, wt_ref, bft_ref, gqt_ref, gkt_ref, gk_ref, gv_ref,
                        w00_ref, b0_ref,
                        k_out, gvn_out, qt_out, kt_out, vt_out, lft_out, ga_out, m_out):
    h = _rms_rows(x_ref[...], gn_ref[...]).astype(BF16)
    t = _dot_nt(wt_ref[...], h)
    qt_out[...] = _rms_heads_t(t[0:D_ATT], gqt_ref[...]) * SCALE
    kt_out[...] = _rms_heads_t(t[D_ATT:2 * D_ATT], gkt_ref[...])
    vt_out[...] = t[2 * D_ATT:3 * D_ATT]
    lft_out[...] = _log_sigmoid(t[3 * D_ATT:3 * D_ATT + N_HEADS] + bft_ref[...])
    k_out[...] = _rms_head_pairs(_dot(h, w_ref[:, W_K:W_K + D_ATT]), gk_ref[...])
    ga_out[...] = _silu(_dot(h, w_ref[:, W_ZA:W_ZA + D_ATT])).astype(BF16)
    u = _dot(h, w_ref[:, W_U:W_U + D_CHK])
    gvn = _rms_groups(_dot(h, w_ref[:, W_GV:W_GV + D_CHK]), gv_ref[...])
    gvn_out[...] = gvn
    gate_c = _silu(_dot(h, w_ref[:, W_ZC:W_ZC + D_CHK]))
    m_out[...] = (u * (w00_ref[...] * gvn + b0_ref[...]) * gate_c).astype(BF16)


def _sample_proj(x, gn, w, wt_s, bft, gqt, gkt, gk, gv, w00, b0):
    n = x.shape[0]
    full = lambda a: pl.BlockSpec(a.shape, lambda i: (0,) * a.ndim)
    args = (x, gn, w, wt_s, bft, gqt, gkt, gk, gv, w00, b0)
    out_shape = (
        jax.ShapeDtypeStruct((n, D_ATT), F32), jax.ShapeDtypeStruct((n, D_CHK), F32),
        jax.ShapeDtypeStruct((D_ATT, n), F32), jax.ShapeDtypeStruct((D_ATT, n), F32),
        jax.ShapeDtypeStruct((D_ATT, n), F32), jax.ShapeDtypeStruct((N_HEADS, n), F32),
        jax.ShapeDtypeStruct((n, D_ATT), BF16), jax.ShapeDtypeStruct((n, D_CHK), BF16),
    )
    return pl.pallas_call(
        _sample_proj_kernel,
        out_shape=out_shape,
        grid=(1,),
        in_specs=[full(a) for a in args],
        out_specs=tuple(pl.BlockSpec(o.shape, lambda i: (0, 0)) for o in out_shape),
        compiler_params=pltpu.CompilerParams(
            dimension_semantics=("arbitrary",), vmem_limit_bytes=VMEM_LIMIT_BYTES),
        name="sample_proj",
    )(*args)


def _page_suffix_matrix():
    pos = np.arange(PAGE)
    later = pos[:, None] > pos[None, :]
    return np.concatenate([later, np.ones((PAGE, PAGE), bool)], axis=1).astype(np.float32)


def _page_suffix_kernel(lf_ref, mat_ref, o_ref):
    a1, a2, a3 = _split3(lf_ref[...])
    mat = mat_ref[...]
    o_ref[...] = _dot(a1, mat) + _dot(a2, mat) + _dot(a3, mat)


def _page_suffix(lf_rows, mat):
    n = lf_rows.shape[0]
    rows = n // PP_STEPS
    assert rows * PP_STEPS == n and rows % 8 == 0
    return pl.pallas_call(
        _page_suffix_kernel,
        out_shape=jax.ShapeDtypeStruct((n, 2 * PAGE), F32),
        grid=(PP_STEPS,),
        in_specs=[pl.BlockSpec((rows, PAGE), lambda i: (i, 0)),
                  pl.BlockSpec((PAGE, 2 * PAGE), lambda i: (0, 0))],
        out_specs=pl.BlockSpec((rows, 2 * PAGE), lambda i: (i, 0)),
        compiler_params=pltpu.CompilerParams(
            dimension_semantics=("arbitrary",), vmem_limit_bytes=VMEM_LIMIT_BYTES),
        name="page_suffix",
    )(lf_rows, mat)


def _layer(xp, xs, cache_k, cache_v, cache_logf, page_table, g_norm, w_in, b_f, g_q, g_k, g_v,
           w_s, b_s, w_out):
    B, S, _ = xp.shape
    nb = xs.shape[0]
    n_phys = cache_k.shape[0]

    cols = lambda a, n: w_in[:, a:a + n]
    wf = cols(_F0, N_HEADS)
    w = jnp.concatenate([cols(_K0, D_ATT), cols(_ZA0, D_ATT), cols(_U0, D_CHK), cols(_GV0, D_CHK),
                         cols(_ZC0, D_CHK)], axis=1).astype(BF16)
    pad8 = jnp.zeros((D_MODEL, 8), F32)
    wt = jnp.concatenate([cols(_Q0, D_ATT), cols(_V0, D_ATT), wf, pad8], axis=1).T.astype(BF16)
    wt_s = jnp.concatenate([cols(_Q0, D_ATT), cols(_K0, D_ATT), cols(_V0, D_ATT), wf, pad8],
                           axis=1).T.astype(BF16)
    wo = w_out.astype(BF16)
    gn = g_norm.reshape(1, D_MODEL)
    bft = b_f.reshape(N_HEADS, 1)
    gq = jnp.tile(g_q, N_HEADS).reshape(1, D_ATT)
    gqt = gq.reshape(D_ATT, 1)
    gk = jnp.tile(g_k, N_HEADS).reshape(1, D_ATT)
    gkt = gk.reshape(D_ATT, 1)
    gv = g_v.reshape(1, D_CHK)
    bsb = jnp.broadcast_to(b_s[:, :, None], (N_GROUPS, CHUNK, CH_GROUP))
    w00 = jnp.repeat(w_s[:, 0, 0], CH_GROUP).reshape(1, D_CHK)
    b0 = jnp.repeat(b_s[:, 0], CH_GROUP).reshape(1, D_CHK)
    utri = jnp.asarray(np.triu(np.ones((TM, TM), np.float32)), BF16)

    k_p, vt_p, lft_p, kb, e, qt, ext, va, ga, m = _prompt_proj(
        xp, gn, w, wt, bft, gqt, gk, gv, w_s, bsb, utri)
    xs2 = xs.reshape(nb, D_MODEL)
    k_s, gvn_s, qts, kts, vts, lfs, ga_s, m_s = _sample_proj(
        xs2, gn, w, wt_s, bft, gqt, gkt, gk, gv, w00, b0)

    lf_rows = jnp.transpose(cache_logf, (0, 2, 1)).reshape(n_phys * N_HEADS, PAGE)
    kt_pages = jnp.transpose(cache_k, (0, 2, 3, 1)).reshape(n_phys, D_ATT, PAGE)
    vt_pages = jnp.transpose(cache_v, (0, 2, 3, 1)).reshape(n_phys, D_ATT, PAGE)
    pp = _page_suffix(lf_rows, jnp.asarray(_page_suffix_matrix(), BF16))
    pp = pp.reshape(n_phys, N_HEADS, 2 * PAGE)
    attn, attn_s_t = _attention(page_table, qt, ext, kb, e, va, qts.T, qts, kts, vts, lfs,
                                kt_pages, vt_pages, pp)

    y_p = _merge(xp, attn, ga, m, wo, 2 * TM)
    y_s = _merge(xs2.reshape(1, nb, D_MODEL), attn_s_t.T.reshape(1, nb, D_ATT).astype(BF16),
                 ga_s.reshape(1, nb, D_ATT), m_s.reshape(1, nb, D_CHK), wo, nb)

    v_p = jnp.transpose(vt_p.reshape(B, N_HEADS, HEAD_DIM, S), (0, 3, 1, 2))
    lf_p = jnp.transpose(lft_p, (0, 2, 1))
    return (y_p, y_s.reshape(nb, 1, D_MODEL),
            k_p.reshape(B, S, N_HEADS, HEAD_DIM), v_p, lf_p,
            k_s.reshape(nb, 1, N_HEADS, HEAD_DIM), vts.T.reshape(nb, 1, N_HEADS, HEAD_DIM),
            lfs.T.reshape(nb, 1, N_HEADS), gvn_s.reshape(nb, 1, N_GROUPS, CH_GROUP))


def kernel(x_prompt, x_sample, cache_k, cache_v, cache_logf, page_table, g_norm, w_in, b_f, g_q, g_k,
           g_v, w_s, b_s, w_out):
    depth = w_in.shape[0]
    assert x_sample.shape[1] == 1, "one new token per decode sequence"
    xp, xs = x_prompt, x_sample
    outs = []
    for l in range(depth):
        res = _layer(xp, xs, cache_k[l], cache_v[l], cache_logf[l], page_table, g_norm[l], w_in[l],
                     b_f[l], g_q[l], g_k[l], g_v[l], w_s[l], b_s[l], w_out[l])
        xp, xs = res[0], res[1]
        outs.append(res[2:])
    stacked = tuple(jnp.stack([o[i] for o in outs]) for i in range(7))
    return (xp, xs) + stacked
```

```python
import functools

import numpy as np
import jax
import jax.numpy as jnp
from jax import lax
from jax.experimental import pallas as pl
from jax.experimental.pallas import tpu as pltpu

F32 = jnp.float32
BF16 = jnp.bfloat16

D_MODEL = 1024
N_HEADS = 8
HEAD_DIM = 64
D_ATT = N_HEADS * HEAD_DIM
N_GROUPS = 4
CH_GROUP = 128
D_CHK = N_GROUPS * CH_GROUP
CHUNK = 128
PAGE = 128
EPS = 1e-6
SCALE = HEAD_DIM ** -0.5
LOG2E = 1.4426950408889634

LANES = 128
VMEM_LIMIT_BYTES = 56 * 1024 * 1024

TM = 512
TQ = 512
TK = 256
KS = TQ
NQB = 4
V_ROWS = 80
X_ROWS = 32
N_SPLIT = 3
SEL_ROWS = N_SPLIT * N_HEADS
PP_STEPS = 4

_Q0, _K0, _V0, _F0 = 0, D_ATT, 2 * D_ATT, 3 * D_ATT
_ZA0 = 3 * D_ATT + N_HEADS
_U0 = _ZA0 + D_ATT
_GV0 = _U0 + D_CHK
_ZC0 = _GV0 + D_CHK
W_K, W_ZA, W_U, W_GV, W_ZC = (i * 512 for i in range(5))
W_COLS = 5 * 512
WT_ROWS = 2 * D_ATT + 16


def _log_sigmoid(x):
    return jnp.minimum(x, 0.0) - jnp.log1p(jnp.exp(-jnp.abs(x)))


def _silu(x):
    return x / (1.0 + jnp.exp(-x))


def _rms_rows(x, gain):
    ms = jnp.sum(x * x, axis=-1, keepdims=True) * (1.0 / x.shape[-1])
    return x * lax.rsqrt(ms + EPS) * gain


def _rms_head_pairs(x, gain):
    outs = []
    for p in range(x.shape[-1] // LANES):
        xp = x[:, p * LANES:(p + 1) * LANES]
        sq = xp * xp
        lo = lax.broadcasted_iota(jnp.int32, xp.shape, 1) < HEAD_DIM
        s_lo = jnp.sum(jnp.where(lo, sq, 0.0), axis=-1, keepdims=True)
        s_hi = jnp.sum(jnp.where(lo, 0.0, sq), axis=-1, keepdims=True)
        r = jnp.where(lo, lax.rsqrt(s_lo * (1.0 / HEAD_DIM) + EPS),
                      lax.rsqrt(s_hi * (1.0 / HEAD_DIM) + EPS))
        outs.append(xp * r * gain[:, p * LANES:(p + 1) * LANES])
    return jnp.concatenate(outs, axis=-1)


def _rms_groups(x, gain):
    outs = []
    for g in range(N_GROUPS):
        xg = x[:, g * CH_GROUP:(g + 1) * CH_GROUP]
        outs.append(_rms_rows(xg, gain[:, g * CH_GROUP:(g + 1) * CH_GROUP]))
    return jnp.concatenate(outs, axis=-1)


def _split3(a):
    a1 = a.astype(BF16)
    r1 = a - a1.astype(F32)
    a2 = r1.astype(BF16)
    a3 = (r1 - a2.astype(F32)).astype(BF16)
    return a1, a2, a3


def _dot(a, b):
    return jnp.dot(a, b, preferred_element_type=F32)


def _dot_nt(a, b):
    return lax.dot_general(a, b, (((1,), (1,)), ((), ())), preferred_element_type=F32)


def _prompt_proj_kernel(x_ref, gn_ref, w_ref, wt_ref, bft_ref, gqt_ref, gk_ref, gv_ref,
                        ws_ref, bsb_ref, utri_ref,
                        k_out, vt_out, lft_out, kb_out, e_out, qt_out, ext_out, va_out, ga_out, m_out,
                        carry_t):
    s_idx = pl.program_id(1)

    @pl.when(s_idx == 0)
    def _():
        carry_t[...] = jnp.zeros_like(carry_t)

    x = x_ref[0]
    h = _rms_rows(x, gn_ref[...]).astype(BF16)

    k = _rms_head_pairs(_dot(h, w_ref[:, W_K:W_K + D_ATT]), gk_ref[...])
    k_out[0] = k
    kb_out[0] = k.astype(BF16)

    t = _dot_nt(wt_ref[...], h)
    qt_out[0] = (_rms_heads_t(t[0:D_ATT], gqt_ref[...]) * (SCALE * LOG2E)).astype(BF16)

    vt = t[D_ATT:2 * D_ATT]
    vt_out[0] = vt
    row16 = lax.broadcasted_iota(jnp.int32, (V_ROWS - HEAD_DIM, TK), 0)
    tail = jnp.where(row16 == 0, 1.0, 0.0)
    for hh in range(N_HEADS):
        for kb in range(TM // TK):
            blk = vt[hh * HEAD_DIM:(hh + 1) * HEAD_DIM, kb * TK:(kb + 1) * TK]
            va_out[0, hh, kb] = jnp.concatenate([blk, tail], axis=0).astype(BF16)

    lft = _log_sigmoid(t[2 * D_ATT:2 * D_ATT + N_HEADS] + bft_ref[...])
    lft_out[0] = lft
    t1, t2, t3 = _split3(lft)
    utri = utri_ref[...]
    ct = carry_t[:, 0:1] + (_dot(t1, utri) + _dot(t2, utri) + _dot(t3, utri))
    carry_t[...] = jnp.broadcast_to(ct[:, TM - 1:TM], carry_t.shape)
    ct1, ct2, ct3 = (piece.astype(F32) for piece in _split3(ct * LOG2E))
    rid = lax.broadcasted_iota(jnp.int32, (X_ROWS, TM), 0)
    for hh in range(N_HEADS):
        sel = jnp.where(rid < SEL_ROWS, jnp.where((rid & (N_HEADS - 1)) == hh, 1.0, 0.0), 0.0)
        ext = jnp.where(rid == SEL_ROWS, ct1[hh:hh + 1],
                        jnp.where(rid == SEL_ROWS + 1, ct2[hh:hh + 1],
                                  jnp.where(rid == SEL_ROWS + 2, ct3[hh:hh + 1], sel)))
        ext_out[0, hh] = ext.astype(BF16)
    rid8 = lax.broadcasted_iota(jnp.int32, (N_HEADS, TM), 0)
    ones = jnp.where(rid8 < N_SPLIT, 1.0, 0.0)
    e_t = jnp.concatenate([-ct1, -ct2, -ct3, ones,
                           jnp.zeros((LANES - SEL_ROWS - N_HEADS, TM), F32)], axis=0)
    e_out[0] = e_t.T.astype(BF16)

    ga_out[0] = _silu(_dot(h, w_ref[:, W_ZA:W_ZA + D_ATT])).astype(BF16)
    u = _dot(h, w_ref[:, W_U:W_U + D_CHK])
    gvn = _rms_groups(_dot(h, w_ref[:, W_GV:W_GV + D_CHK]), gv_ref[...]).astype(BF16)
    gate_c = _silu(_dot(h, w_ref[:, W_ZC:W_ZC + D_CHK]))
    tri = (lax.broadcasted_iota(jnp.int32, (CHUNK, CHUNK), 1)
           <= lax.broadcasted_iota(jnp.int32, (CHUNK, CHUNK), 0))
    for g in range(N_GROUPS):
        wg = jnp.where(tri, ws_ref[g], 0.0).astype(BF16)
        cs = slice(g * CH_GROUP, (g + 1) * CH_GROUP)
        for cc in range(TM // CHUNK):
            rs = slice(cc * CHUNK, (cc + 1) * CHUNK)
            mix = _dot(wg, gvn[rs, cs]) + bsb_ref[g]
            m_out[0, rs, cs] = (u[rs, cs] * mix * gate_c[rs, cs]).astype(BF16)


def _prompt_proj(x, gn, w, wt, bft, gqt, gk, gv, ws, bsb, utri):
    B, S, _ = x.shape
    nk = S // TK
    const2 = lambda shape: pl.BlockSpec(shape, lambda b, s: (0, 0))
    const3 = lambda shape: pl.BlockSpec(shape, lambda b, s: (0, 0, 0))
    row_blk = lambda width: pl.BlockSpec((1, TM, width), lambda b, s: (b, s, 0))
    col_blk = lambda rows: pl.BlockSpec((1, rows, TM), lambda b, s: (b, 0, s))
    out_shape = (
        jax.ShapeDtypeStruct((B, S, D_ATT), F32),
        jax.ShapeDtypeStruct((B, D_ATT, S), F32),
        jax.ShapeDtypeStruct((B, N_HEADS, S), F32),
        jax.ShapeDtypeStruct((B, S, D_ATT), BF16),
        jax.ShapeDtypeStruct((B, S, LANES), BF16),
        jax.ShapeDtypeStruct((B, D_ATT, S), BF16),
        jax.ShapeDtypeStruct((B, N_HEADS, X_ROWS, S), BF16),
        jax.ShapeDtypeStruct((B, N_HEADS, nk, V_ROWS, TK), BF16),
        jax.ShapeDtypeStruct((B, S, D_ATT), BF16),
        jax.ShapeDtypeStruct((B, S, D_CHK), BF16),
    )
    out_specs = (
        row_blk(D_ATT), col_blk(D_ATT), col_blk(N_HEADS), row_blk(D_ATT), row_blk(LANES),
        col_blk(D_ATT),
        pl.BlockSpec((1, N_HEADS, X_ROWS, TM), lambda b, s: (b, 0, 0, s)),
        pl.BlockSpec((1, N_HEADS, TM // TK, V_ROWS, TK), lambda b, s: (b, 0, s, 0, 0)),
        row_blk(D_ATT), row_blk(D_CHK),
    )
    in_specs = [
        row_blk(D_MODEL),
        const2((1, D_MODEL)),
        const2((D_MODEL, W_COLS)),
        const2((WT_ROWS, D_MODEL)),
        const2((N_HEADS, 1)),
        const2((D_ATT, 1)),
        const2((1, D_ATT)),
        const2((1, D_CHK)),
        const3((N_GROUPS, CHUNK, CHUNK)),
        const3((N_GROUPS, CHUNK, CH_GROUP)),
        const2((TM, TM)),
    ]
    return pl.pallas_call(
        _prompt_proj_kernel,
        out_shape=out_shape,
        grid=(B, S // TM),
        in_specs=in_specs,
        out_specs=out_specs,
        scratch_shapes=[pltpu.VMEM((N_HEADS, LANES), F32)],
        compiler_params=pltpu.CompilerParams(
            dimension_semantics=("arbitrary", "arbitrary"), vmem_limit_bytes=VMEM_LIMIT_BYTES),
        name="prompt_proj",
    )(x, gn, w, wt, bft, gqt, gk, gv, ws, bsb, utri)


def _head_sum(x):
    return jnp.sum(x.reshape(N_HEADS, HEAD_DIM, x.shape[-1]), axis=1)


def _per_dim(x):
    return jnp.broadcast_to(x[:, None, :], (N_HEADS, HEAD_DIM, x.shape[-1])).reshape(D_ATT, x.shape[-1])


def _page_copies(pt_ref, seq, slot, kt_hbm, vt_hbm, pp_hbm, kbuf, vbuf, ppbuf, sems):
    copies = []
    for j in range(kbuf.shape[1]):
        page = pt_ref[seq, j]
        copies.append(pltpu.make_async_copy(kt_hbm.at[page], kbuf.at[slot, j], sems.at[slot, 0]))
        copies.append(pltpu.make_async_copy(vt_hbm.at[page], vbuf.at[slot, j], sems.at[slot, 1]))
        copies.append(pltpu.make_async_copy(pp_hbm.at[page], ppbuf.at[slot, j], sems.at[slot, 2]))
    return copies


def _decode_past(seq, slot, q_ref, lfs_ref, kbuf, vbuf, ppbuf, os_ref, st_ref):
    n_pages = kbuf.shape[1]
    n_seq = os_ref.shape[-1]
    lane_is_seq = lambda rows: lax.broadcasted_iota(jnp.int32, (rows, n_seq), 1) == seq
    own = (lax.broadcasted_iota(jnp.int32, (N_HEADS, D_ATT), 0)
           == lax.broadcasted_iota(jnp.int32, (N_HEADS, D_ATT), 1) // HEAD_DIM)
    qbd = jnp.where(own, jnp.broadcast_to(q_ref[pl.ds(seq, 1), :], (N_HEADS, D_ATT)), 0.0)
    lf_new = jnp.sum(jnp.where(lane_is_seq(N_HEADS), lfs_ref[...], 0.0), axis=1, keepdims=True)

    off = jnp.broadcast_to(lf_new, (N_HEADS, PAGE))
    scores = [None] * n_pages
    for p in range(n_pages - 1, -1, -1):
        scores[p] = _dot(qbd, kbuf[slot, p]) + ppbuf[slot, p, :, 0:PAGE] + off
        off = off + ppbuf[slot, p, :, PAGE:2 * PAGE]
    top = scores[0]
    for p in range(1, n_pages):
        top = jnp.maximum(top, scores[p])
    m = jnp.max(top, axis=-1, keepdims=True)
    probs = [jnp.exp(s - m) for s in scores]
    p_sum = probs[0]
    for p in range(1, n_pages):
        p_sum = p_sum + probs[p]
    cols = []
    for hh in range(N_HEADS):
        rows = slice(hh * HEAD_DIM, (hh + 1) * HEAD_DIM)
        acc = vbuf[slot, 0, rows, :] * probs[0][hh:hh + 1, :]
        for p in range(1, n_pages):
            acc = acc + vbuf[slot, p, rows, :] * probs[p][hh:hh + 1, :]
        cols.append(jnp.sum(acc, axis=-1, keepdims=True))
    os_ref[...] = jnp.where(lane_is_seq(D_ATT), jnp.concatenate(cols, axis=0), os_ref[...])
    stats = jnp.concatenate([m, jnp.sum(p_sum, axis=-1, keepdims=True)], axis=0)
    st_ref[...] = jnp.where(lane_is_seq(2 * N_HEADS), stats, st_ref[...])


def _decode_finish(qts_ref, kts_ref, vts_ref, os_ref, st_ref):
    s_self = _head_sum(qts_ref[...] * kts_ref[...])
    m_past, l_past = st_ref[0:N_HEADS, :], st_ref[N_HEADS:2 * N_HEADS, :]
    m = jnp.maximum(m_past, s_self)
    w_past, p_self = jnp.exp(m_past - m), jnp.exp(s_self - m)
    os_ref[...] = ((os_ref[...] * _per_dim(w_past) + _per_dim(p_self) * vts_ref[...])
                   / _per_dim(l_past * w_past + p_self))


def _query_block(i, qt, ext, kb_ref, e_ref, va_ref, wq_ref, s_ref, p_ref):
    last_tile = (i * TQ) // KS
    n_tiles = last_tile + 1

    wq_ref[...] = jnp.zeros_like(wq_ref)
    wq_ref[0:HEAD_DIM, 0:TQ] = qt[0:HEAD_DIM, :]
    wq_ref[HEAD_DIM:2 * HEAD_DIM, TQ:2 * TQ] = qt[HEAD_DIM:2 * HEAD_DIM, :]
    wq_ref[LANES:LANES + X_ROWS, 0:TQ] = ext[0]
    wq_ref[LANES:LANES + X_ROWS, TQ:2 * TQ] = ext[1]

    visit = lambda k: jnp.where(k == 0, last_tile, k - 1)

    def scores(tile):
        off = pl.multiple_of(tile * KS, KS)
        ke = jnp.concatenate([kb_ref[0, pl.ds(off, KS), :], e_ref[0, pl.ds(off, KS), :]], axis=1)
        s = _dot(ke, wq_ref[...])
        s_ref[...] = s
        return jnp.max(s, axis=0, keepdims=True)

    def probs(m, s, blk_max, valid=None):
        m_new = jnp.maximum(m, blk_max)
        alpha = jnp.exp2(m - m_new)
        if valid is not None:
            alpha = jnp.where(valid, alpha, 1.0)
            p_ref[...] = jnp.exp2(s - jnp.where(valid, m_new, jnp.inf)).astype(BF16)
            return jnp.where(valid, m_new, m), alpha
        p_ref[...] = jnp.exp2(s - m_new).astype(BF16)
        return m_new, alpha

    def accumulate(tile, alpha, accs):
        out = []
        for hh in range(2):
            cols = slice(hh * TQ, (hh + 1) * TQ)
            acc = alpha[:, cols] * accs[hh]
            for kk in range(KS // TK):
                acc = acc + _dot(va_ref[0, hh, tile * (KS // TK) + kk], p_ref[kk * TK:(kk + 1) * TK, cols])
            out.append(acc)
        return tuple(out)

    scores(last_tile)
    krow = lax.broadcasted_iota(jnp.int32, (KS, 2 * TQ), 0) + (last_tile * KS - i * TQ)
    qcol = lax.broadcasted_iota(jnp.int32, (KS, 2 * TQ), 1) & (TQ - 1)
    s_diag = jnp.where(krow <= qcol, s_ref[...], -jnp.inf)
    zero = jnp.zeros((V_ROWS, TQ), F32)
    m, alpha = probs(jnp.full((1, 2 * TQ), -jnp.inf, F32), s_diag, jnp.max(s_diag, axis=0, keepdims=True))
    state0 = (m, alpha, (zero, zero), scores(0))

    def body(k, carry):
        m, alpha, accs, blk_max = carry
        accs = accumulate(visit(k), alpha, accs)
        m, alpha = probs(m, s_ref[...], blk_max)
        return m, alpha, accs, scores(k + 1)

    trips = jnp.maximum(n_tiles - 2, 0)

    def drain(state):
        m, alpha, accs, blk_max = state
        accs = accumulate(visit(jnp.maximum(n_tiles - 2, 0)), alpha, accs)
        m, alpha = probs(m, s_ref[...], blk_max, valid=n_tiles >= 2)
        acc0, acc1 = accumulate(visit(n_tiles - 1), alpha, accs)
        o0 = acc0[0:HEAD_DIM] / acc0[HEAD_DIM:HEAD_DIM + 1]
        o1 = acc1[0:HEAD_DIM] / acc1[HEAD_DIM:HEAD_DIM + 1]
        return jnp.concatenate([o0, o1], axis=0).T.astype(BF16)

    return state0, body, trips, drain


def _attention_kernel(n_seq, pt_ref, qt_ref, ext_ref, kb_ref, e_ref, va_ref,
                      qs_ref, qts_ref, kts_ref, vts_ref, lfs_ref, kt_hbm, vt_hbm, pp_hbm,
                      o_ref, os_ref, wq_refs, s_refs, p_refs, kbuf, vbuf, ppbuf, st_ref, sems):
    step = (pl.program_id(0) * pl.num_programs(1) + pl.program_id(1)) * pl.num_programs(2) + pl.program_id(2)
    copies = functools.partial(_page_copies, pt_ref, kt_hbm=kt_hbm, vt_hbm=vt_hbm, pp_hbm=pp_hbm,
                               kbuf=kbuf, vbuf=vbuf, ppbuf=ppbuf, sems=sems)

    @pl.when(step == 0)
    def _():
        os_ref[...] = jnp.zeros_like(os_ref)
        st_ref[...] = jnp.zeros_like(st_ref)
        for buf in range(2):
            for n, c in enumerate(copies(buf, buf)):
                c.start(priority=n % 2)

    pending = []
    for r in range(NQB // 2):
        seqs = [step * NQB + 2 * r + buf for buf in range(2)]
        for buf in range(2):
            for c in copies(seqs[buf], buf):
                c.wait()
        for buf in range(2):
            _decode_past(seqs[buf], buf, qs_ref, lfs_ref, kbuf, vbuf, ppbuf, os_ref, st_ref)
        for buf in range(2):
            for n, c in enumerate(copies(jnp.minimum(seqs[buf] + 2, n_seq - 2 + buf), buf)):
                c.start(priority=n % 2)
        for state, drain, rows in pending:
            o_ref[0, rows, :] = drain(state)
        blocks = []
        for buf in range(2):
            j = 2 * r + buf
            rows = slice(j * TQ, (j + 1) * TQ)
            scr = 2 * (r % 2) + buf
            blocks.append(_query_block(
                pl.program_id(2) * NQB + j, qt_ref[0, :, rows], ext_ref[0, :, :, rows],
                kb_ref, e_ref, va_ref, wq_refs.at[scr], s_refs.at[scr], p_refs.at[scr]) + (rows,))
        (state_a, body_a, trips_a, drain_a, rows_a), (state_b, body_b, trips_b, drain_b, rows_b) = blocks
        both = lambda k, st: (body_a(k, st[0]), body_b(k, st[1]))
        pairs = trips_a // 2
        states = lax.fori_loop(0, pairs, lambda kk, st: both(2 * kk + 1, both(2 * kk, st)),
                               (state_a, state_b))

        def tail(k, st):
            sa, sb = both(k, st)
            return sa, body_b(k + 1, sb)

        state_a, state_b = lax.fori_loop(2 * pairs, trips_a, tail, states)
        pending = [(state_a, drain_a, rows_a), (state_b, drain_b, rows_b)]
    for state, drain, rows in pending:
        o_ref[0, rows, :] = drain(state)

    @pl.when(step * NQB + NQB == n_seq)
    def _():
        for buf in range(2):
            for c in copies(n_seq - 2 + buf, buf):
                c.wait()
        _decode_finish(qts_ref, kts_ref, vts_ref, os_ref, st_ref)


def _attention(page_table, qt, ext, kb, e, va, qs, qts, kts, vts, lfs, kt_pages, vt_pages, pp):
    B, _, S = qt.shape
    nk = S // TK
    n_seq, n_pages = page_table.shape
    tq = NQB * TQ
    grid = (B, N_HEADS // 2, S // tq)
    assert NQB % 2 == 0 and n_seq == grid[0] * grid[1] * grid[2] * NQB, "one decode sequence per query block"
    resident = lambda a: pl.BlockSpec(a.shape, lambda b, p, g, pt: (0, 0))
    in_hbm = pl.BlockSpec(memory_space=pl.ANY)
    in_specs = [
        pl.BlockSpec((1, LANES, tq), lambda b, p, g, pt: (b, p, g)),
        pl.BlockSpec((1, 2, X_ROWS, tq), lambda b, p, g, pt: (b, p, 0, g)),
        pl.BlockSpec((1, S, LANES), lambda b, p, g, pt: (b, 0, p)),
        pl.BlockSpec((1, S, LANES), lambda b, p, g, pt: (b, 0, 0)),
        pl.BlockSpec((1, 2, nk, V_ROWS, TK), lambda b, p, g, pt: (b, p, 0, 0, 0)),
        resident(qs), resident(qts), resident(kts), resident(vts), resident(lfs),
        in_hbm, in_hbm, in_hbm]
    grid_spec = pltpu.PrefetchScalarGridSpec(
        num_scalar_prefetch=1, grid=grid, in_specs=in_specs,
        out_specs=(pl.BlockSpec((1, tq, LANES), lambda b, p, g, pt: (b, g, p)), resident(qts)),
        scratch_shapes=[pltpu.VMEM((4, 2 * LANES, 2 * TQ), BF16), pltpu.VMEM((4, KS, 2 * TQ), F32),
                        pltpu.VMEM((4, KS, 2 * TQ), BF16),
                        pltpu.VMEM((2, n_pages, D_ATT, PAGE), F32), pltpu.VMEM((2, n_pages, D_ATT, PAGE), F32),
                        pltpu.VMEM((2, n_pages, N_HEADS, 2 * PAGE), F32),
                        pltpu.VMEM((2 * N_HEADS, n_seq), F32), pltpu.SemaphoreType.DMA((2, 3))])
    return pl.pallas_call(
        functools.partial(_attention_kernel, n_seq),
        out_shape=(jax.ShapeDtypeStruct((B, S, D_ATT), BF16), jax.ShapeDtypeStruct((D_ATT, n_seq), F32)),
        grid_spec=grid_spec,
        compiler_params=pltpu.CompilerParams(
            dimension_semantics=("arbitrary", "arbitrary", "arbitrary"),
            vmem_limit_bytes=VMEM_LIMIT_BYTES),
        name="attention",
    )(page_table, qt, ext, kb, e, va, qs, qts, kts, vts, lfs, kt_pages, vt_pages, pp)


def _merge_kernel(x_ref, a_ref, ga_ref, m_ref, wo_ref, y_ref):
    a = (a_ref[0].astype(F32) * ga_ref[0].astype(F32)).astype(BF16)
    y_ref[0] = (x_ref[0] + _dot(a, wo_ref[0:D_ATT, :]) + _dot(m_ref[0], wo_ref[D_ATT:D_ATT + D_CHK, :]))


def _merge(x, a, ga, m, wo, tm):
    B, S, _ = x.shape
    row_blk = lambda width: pl.BlockSpec((1, tm, width), lambda b, s: (b, s, 0))
    return pl.pallas_call(
        _merge_kernel,
        out_shape=jax.ShapeDtypeStruct((B, S, D_MODEL), F32),
        grid=(B, S // tm),
        in_specs=[row_blk(D_MODEL), row_blk(D_ATT), row_blk(D_ATT), row_blk(D_CHK),
                  pl.BlockSpec((D_ATT + D_CHK, D_MODEL), lambda b, s: (0, 0))],
        out_specs=row_blk(D_MODEL),
        compiler_params=pltpu.CompilerParams(
            dimension_semantics=("arbitrary", "arbitrary"), vmem_limit_bytes=VMEM_LIMIT_BYTES),
        name="merge",
    )(x, a, ga, m, wo)


def _rms_heads_t(xt, gain_col):
    n = xt.shape[-1]
    x3 = xt.reshape(N_HEADS, HEAD_DIM, n)
    ssq = jnp.sum(x3 * x3, axis=1, keepdims=True)
    return (x3 * lax.rsqrt(ssq * (1.0 / HEAD_DIM) + EPS)).reshape(D_ATT, n) * gain_col


def _sample_proj_kernel(x_ref, gn_ref, w_ref, wt_ref, bft_ref, gqt_ref, gkt_ref, gk_ref, gv_ref,
                        w00_ref, b0_ref,
                        k_out, gvn_out, qt_out, kt_out, vt_out, lft_out, ga_out, m_out):
    h = _rms_rows(x_ref[...], gn_ref[...]).astype(BF16)
    t = _dot_nt(wt_ref[...], h)
    qt_out[...] = _rms_heads_t(t[0:D_ATT], gqt_ref[...]) * SCALE
    kt_out[...] = _rms_heads_t(t[D_ATT:2 * D_ATT], gkt_ref[...])
    vt_out[...] = t[2 * D_ATT:3 * D_ATT]
    lft_out[...] = _log_sigmoid(t[3 * D_ATT:3 * D_ATT + N_HEADS] + bft_ref[...])
    k_out[...] = _rms_head_pairs(_dot(h, w_ref[:, W_K:W_K + D_ATT]), gk_ref[...])
    ga_out[...] = _silu(_dot(h, w_ref[:, W_ZA:W_ZA + D_ATT])).astype(BF16)
    u = _dot(h, w_ref[:, W_U:W_U + D_CHK])
    gvn = _rms_groups(_dot(h, w_ref[:, W_GV:W_GV + D_CHK]), gv_ref[...])
    gvn_out[...] = gvn
    gate_c = _silu(_dot(h, w_ref[:, W_ZC:W_ZC + D_CHK]))
    m_out[...] = (u * (w00_ref[...] * gvn + b0_ref[...]) * gate_c).astype(BF16)


def _sample_proj(x, gn, w, wt_s, bft, gqt, gkt, gk, gv, w00, b0):
    n = x.shape[0]
    full = lambda a: pl.BlockSpec(a.shape, lambda i: (0,) * a.ndim)
    args = (x, gn, w, wt_s, bft, gqt, gkt, gk, gv, w00, b0)
    out_shape = (
        jax.ShapeDtypeStruct((n, D_ATT), F32), jax.ShapeDtypeStruct((n, D_CHK), F32),
        jax.ShapeDtypeStruct((D_ATT, n), F32), jax.ShapeDtypeStruct((D_ATT, n), F32),
        jax.ShapeDtypeStruct((D_ATT, n), F32), jax.ShapeDtypeStruct((N_HEADS, n), F32),
        jax.ShapeDtypeStruct((n, D_ATT), BF16), jax.ShapeDtypeStruct((n, D_CHK), BF16),
    )
    return pl.pallas_call(
        _sample_proj_kernel,
        out_shape=out_shape,
        grid=(1,),
        in_specs=[full(a) for a in args],
        out_specs=tuple(pl.BlockSpec(o.shape, lambda i: (0, 0)) for o in out_shape),
        compiler_params=pltpu.CompilerParams(
            dimension_semantics=("arbitrary",), vmem_limit_bytes=VMEM_LIMIT_BYTES),
        name="sample_proj",
    )(*args)


def _page_suffix_matrix():
    pos = np.arange(PAGE)
    later = pos[:, None] > pos[None, :]
    return np.concatenate([later, np.ones((PAGE, PAGE), bool)], axis=1).astype(np.float32)


def _page_suffix_kernel(lf_ref, mat_ref, o_ref):
    a1, a2, a3 = _split3(lf_ref[...])
    mat = mat_ref[...]
    o_ref[...] = _dot(a1, mat) + _dot(a2, mat) + _dot(a3, mat)


def _page_suffix(lf_rows, mat):
    n = lf_rows.shape[0]
    rows = n // PP_STEPS
    assert rows * PP_STEPS == n and rows % 8 == 0
    return pl.pallas_call(
        _page_suffix_kernel,
        out_shape=jax.ShapeDtypeStruct((n, 2 * PAGE), F32),
        grid=(PP_STEPS,),
        in_specs=[pl.BlockSpec((rows, PAGE), lambda i: (i, 0)),
                  pl.BlockSpec((PAGE, 2 * PAGE), lambda i: (0, 0))],
        out_specs=pl.BlockSpec((rows, 2 * PAGE), lambda i: (i, 0)),
        compiler_params=pltpu.CompilerParams(
            dimension_semantics=("arbitrary",), vmem_limit_bytes=VMEM_LIMIT_BYTES),
        name="page_suffix",
    )(lf_rows, mat)


def _layer(xp, xs, cache_k, cache_v, cache_logf, page_table, g_norm, w_in, b_f, g_q, g_k, g_v,
           w_s, b_s, w_out):
    B, S, _ = xp.shape
    nb = xs.shape[0]
    n_phys = cache_k.shape[0]

    cols = lambda a, n: w_in[:, a:a + n]
    wf = cols(_F0, N_HEADS)
    w = jnp.concatenate([cols(_K0, D_ATT), cols(_ZA0, D_ATT), cols(_U0, D_CHK), cols(_GV0, D_CHK),
                         cols(_ZC0, D_CHK)], axis=1).astype(BF16)
    pad8 = jnp.zeros((D_MODEL, 8), F32)
    wt = jnp.concatenate([cols(_Q0, D_ATT), cols(_V0, D_ATT), wf, pad8], axis=1).T.astype(BF16)
    wt_s = jnp.concatenate([cols(_Q0, D_ATT), cols(_K0, D_ATT), cols(_V0, D_ATT), wf, pad8],
                           axis=1).T.astype(BF16)
    wo = w_out.astype(BF16)
    gn = g_norm.reshape(1, D_MODEL)
    bft = b_f.reshape(N_HEADS, 1)
    gq = jnp.tile(g_q, N_HEADS).reshape(1, D_ATT)
    gqt = gq.reshape(D_ATT, 1)
    gk = jnp.tile(g_k, N_HEADS).reshape(1, D_ATT)
    gkt = gk.reshape(D_ATT, 1)
    gv = g_v.reshape(1, D_CHK)
    bsb = jnp.broadcast_to(b_s[:, :, None], (N_GROUPS, CHUNK, CH_GROUP))
    w00 = jnp.repeat(w_s[:, 0, 0], CH_GROUP).reshape(1, D_CHK)
    b0 = jnp.repeat(b_s[:, 0], CH_GROUP).reshape(1, D_CHK)
    utri = jnp.asarray(np.triu(np.ones((TM, TM), np.float32)), BF16)

    k_p, vt_p, lft_p, kb, e, qt, ext, va, ga, m = _prompt_proj(
        xp, gn, w, wt, bft, gqt, gk, gv, w_s, bsb, utri)
    xs2 = xs.reshape(nb, D_MODEL)
    k_s, gvn_s, qts, kts, vts, lfs, ga_s, m_s = _sample_proj(
        xs2, gn, w, wt_s, bft, gqt, gkt, gk, gv, w00, b0)

    lf_rows = jnp.transpose(cache_logf, (0, 2, 1)).reshape(n_phys * N_HEADS, PAGE)
    kt_pages = jnp.transpose(cache_k, (0, 2, 3, 1)).reshape(n_phys, D_ATT, PAGE)
    vt_pages = jnp.transpose(cache_v, (0, 2, 3, 1)).reshape(n_phys, D_ATT, PAGE)
    pp = _page_suffix(lf_rows, jnp.asarray(_page_suffix_matrix(), BF16))
    pp = pp.reshape(n_phys, N_HEADS, 2 * PAGE)
    attn, attn_s_t = _attention(page_table, qt, ext, kb, e, va, qts.T, qts, kts, vts, lfs,
                                kt_pages, vt_pages, pp)

    y_p = _merge(xp, attn, ga, m, wo, 2 * TM)
    y_s = _merge(xs2.reshape(1, nb, D_MODEL), attn_s_t.T.reshape(1, nb, D_ATT).astype(BF16),
                 ga_s.reshape(1, nb, D_ATT), m_s.reshape(1, nb, D_CHK), wo, nb)

    v_p = jnp.transpose(vt_p.reshape(B, N_HEADS, HEAD_DIM, S), (0, 3, 1, 2))
    lf_p = jnp.transpose(lft_p, (0, 2, 1))
    return (y_p, y_s.reshape(nb, 1, D_MODEL),
            k_p.reshape(B, S, N_HEADS, HEAD_DIM), v_p, lf_p,
            k_s.reshape(nb, 1, N_HEADS, HEAD_DIM), vts.T.reshape(nb, 1, N_HEADS, HEAD_DIM),
            lfs.T.reshape(nb, 1, N_HEADS), gvn_s.reshape(nb, 1, N_GROUPS, CH_GROUP))


def kernel(x_prompt, x_sample, cache_k, cache_v, cache_logf, page_table, g_norm, w_in, b_f, g_q, g_k,
           g_v, w_s, b_s, w_out):
    depth = w_in.shape[0]
    assert x_sample.shape[1] == 1, "one new token per decode sequence"
    xp, xs = x_prompt, x_sample
    outs = []
    for l in range(depth):
        res = _layer(xp, xs, cache_k[l], cache_v[l], cache_logf[l], page_table, g_norm[l], w_in[l],
                     b_f[l], g_q[l], g_k[l], g_v[l], w_s[l], b_s[l], w_out[l])
        xp, xs = res[0], res[1]
        outs.append(res[2:])
    stacked = tuple(jnp.stack([o[i] for o in outs]) for i in range(7))
    return (xp, xs) + stacked
```
